```python
import math
import jax, jax.numpy as jnp
from jax import lax
import numpy as np

D_MODEL = 1024
BATCH = 16
SEQ = 2048
DEPTH = 2

GRID_W = 64
CTX_LEN = 256

FOURIER_WIDTH = 256
FOURIER_GROUPS = 4
FOURIER_GROUP_DIM = FOURIER_WIDTH // FOURIER_GROUPS
ATTN_WIDTH = D_MODEL - FOURIER_WIDTH
DIFF_HEAD_DIM = 64
DIFF_HEADS = ATTN_WIDTH // (2 * DIFF_HEAD_DIM)
DIFF_SUBHEADS = 2 * DIFF_HEADS
DIFF_V_DIM = 2 * DIFF_HEAD_DIM
MIX_WIDTH = FOURIER_WIDTH + ATTN_WIDTH
IN_WIDTH = FOURIER_WIDTH + 3 * ATTN_WIDTH
Q_OFF = FOURIER_WIDTH
K_OFF = FOURIER_WIDTH + ATTN_WIDTH
V_OFF = FOURIER_WIDTH + 2 * ATTN_WIDTH
Q_BLOCK = 128
ROPE_BASE = 10000.0
ROPE_AXIS_DIM = DIFF_HEAD_DIM // 2

N_EXPERTS = 16
N_GROUPS = 4
EXPERTS_PER_GROUP = N_EXPERTS // N_GROUPS
TOP_K = 2
D_EXPERT = D_MODEL // 2

ALPHA = (2 * DEPTH) ** 0.25
BETA = (8 * DEPTH) ** -0.25
LN_EPS = 1e-5

kernel_name = "hybrid_fourier_diffattn_grouped_moe_dit"


def layer_norm(x, g=None, b=None):
    xf = x.astype(jnp.float32)
    mu = jnp.mean(xf, axis=-1, keepdims=True)
    var = jnp.mean(jnp.square(xf - mu), axis=-1, keepdims=True)
    y = (xf - mu) * lax.rsqrt(var + LN_EPS)
    if g is not None:
        y = y * g.astype(jnp.float32) + b.astype(jnp.float32)
    return y.astype(x.dtype)


def modulate(x, shift, scale):
    return x * (1 + scale) + shift


def fourier_mix(u, w_f):
    B, T, _ = u.shape
    ug = u.reshape(B, T, FOURIER_GROUPS, FOURIER_GROUP_DIM).astype(jnp.float32)
    f = jnp.fft.fft2(ug, axes=(1, 3)).real * (1.0 / math.sqrt(T * FOURIER_GROUP_DIM))
    f = f.astype(u.dtype)
    return jnp.einsum('btgc,gcd->btgd', f, w_f).reshape(B, T, FOURIER_WIDTH)


def rope1d(x, ang):
    half = x.shape[-1] // 2
    cos = jnp.cos(ang)[:, None, :].astype(x.dtype)
    sin = jnp.sin(ang)[:, None, :].astype(x.dtype)
    x1, x2 = x[..., :half], x[..., half:]
    return jnp.concatenate([x1 * cos - x2 * sin, x2 * cos + x1 * sin], axis=-1)


def rope2d(x, ang_row, ang_col):
    return jnp.concatenate([rope1d(x[..., :ROPE_AXIS_DIM], ang_row),
                            rope1d(x[..., ROPE_AXIS_DIM:], ang_col)], axis=-1)


def diff_weights(s, lam):
    p = jax.nn.softmax(s, axis=-1)
    p = p.reshape(p.shape[0], DIFF_HEADS, 2, p.shape[2], p.shape[3])
    return p[:, :, 0] - lam * p[:, :, 1]


def diff_attn_latent(q, k, v, kc, vc, lam):
    B, S = q.shape[0], q.shape[1]
    keys = jnp.concatenate([kc, k], axis=1).transpose(0, 2, 1, 3)
    vals = jnp.concatenate([vc, v], axis=1).transpose(0, 2, 1, 3)
    n_blk = S // Q_BLOCK
    qb = q.reshape(B, n_blk, Q_BLOCK, DIFF_SUBHEADS, DIFF_HEAD_DIM).transpose(1, 0, 3, 2, 4)
    scale = DIFF_HEAD_DIM ** -0.5

    def attend(qblk):
        s = jnp.einsum('bhqd,bhkd->bhqk', qblk, keys).astype(jnp.float32) * scale
        a = diff_weights(s, lam)
        return jnp.einsum('bhqk,bhkv->bhqv', a.astype(vals.dtype), vals)

    o = lax.map(attend, qb)
    return o.transpose(1, 0, 3, 2, 4).reshape(B, S, DIFF_HEADS, DIFF_V_DIM)


def diff_attn_context(qc, kc, vc, lam):
    s = jnp.einsum('bqhd,bkhd->bhqk', qc, kc).astype(jnp.float32) * (DIFF_HEAD_DIM ** -0.5)
    a = diff_weights(s, lam)
    return jnp.einsum('bhqk,bkhv->bqhv', a.astype(vc.dtype), vc)


def head_norm(o, g, lam_init):
    of = o.astype(jnp.float32)
    of = of * lax.rsqrt(jnp.mean(jnp.square(of), axis=-1, keepdims=True) + LN_EPS)
    of = of * g.astype(jnp.float32) * (1.0 - lam_init)
    return of.astype(o.dtype).reshape(o.shape[0], o.shape[1], ATTN_WIDTH)


def moe_ffn(h, w_router, router_bias, w_gate, w_up, w_down):
    scores = jax.nn.sigmoid((h @ w_router).astype(jnp.float32))
    sel = scores + router_bias.astype(jnp.float32)
    grp_sel = sel.reshape(sel.shape[:-1] + (N_GROUPS, EXPERTS_PER_GROUP))
    grp_score = lax.top_k(grp_sel, TOP_K)[0].sum(-1)
    best_grp = jnp.argmax(grp_score, axis=-1)
    in_grp = (jnp.arange(N_EXPERTS) // EXPERTS_PER_GROUP) == best_grp[..., None]
    masked = jnp.where(in_grp, sel, -jnp.inf)
    _, idx = lax.top_k(masked, TOP_K)
    w = jnp.take_along_axis(scores, idx, axis=-1)
    w = w / jnp.sum(w, axis=-1, keepdims=True)
    combine = jnp.sum(jax.nn.one_hot(idx, N_EXPERTS, dtype=jnp.float32) * w[..., None], axis=-2)
    combine = combine.astype(h.dtype)
    y = jnp.zeros_like(h)
    for e in range(N_EXPERTS):
        a = jax.nn.silu(h @ w_gate[e]) * (h @ w_up[e])
        y = y + combine[..., e:e + 1] * (a @ w_down[e])
    return y


def setup_inputs(seed: int = 0) -> dict:
    key = jax.random.key(seed)
    ks = jax.random.split(key, 20)

    def n(k, shape, s):
        return jax.random.normal(k, shape, jnp.float32) * s

    C = FOURIER_GROUP_DIM
    return {
        "x": n(ks[0], (BATCH, SEQ, D_MODEL), 1.0),
        "c": n(ks[1], (BATCH, D_MODEL), 1.0),
        "ctx": n(ks[2], (BATCH, CTX_LEN, D_MODEL), 1.0),
        "c_ctx": n(ks[3], (D_MODEL,), 1.0),
        "w_mod": n(ks[4], (DEPTH, D_MODEL, 6 * D_MODEL), D_MODEL ** -0.5),
        "b_mod": n(ks[5], (DEPTH, 6 * D_MODEL), 0.02),
        "w_in": n(ks[6], (DEPTH, D_MODEL, IN_WIDTH), D_MODEL ** -0.5),
        "w_fourier": n(ks[7], (DEPTH, FOURIER_GROUPS, C, C), C ** -0.5),
        "lam_qk": n(ks[8], (DEPTH, 4, DIFF_HEAD_DIM), 0.1),
        "subln_g": 1.0 + n(ks[9], (DEPTH, DIFF_V_DIM), 0.02),
        "w_out": n(ks[10], (DEPTH, MIX_WIDTH, D_MODEL), MIX_WIDTH ** -0.5 * BETA),
        "ln_attn_g": 1.0 + n(ks[11], (DEPTH, D_MODEL), 0.02),
        "ln_attn_b": n(ks[12], (DEPTH, D_MODEL), 0.02),
        "ln_ffn_g": 1.0 + n(ks[13], (DEPTH, D_MODEL), 0.02),
        "ln_ffn_b": n(ks[14], (DEPTH, D_MODEL), 0.02),
        "w_router": n(ks[15], (D_MODEL, N_EXPERTS), D_MODEL ** -0.5),
        "router_bias": n(ks[16], (N_EXPERTS,), 0.01),
        "w_gate": n(ks[17], (DEPTH, N_EXPERTS, D_MODEL, D_EXPERT), D_MODEL ** -0.5),
        "w_up": n(ks[18], (DEPTH, N_EXPERTS, D_MODEL, D_EXPERT), D_MODEL ** -0.5),
        "w_down": n(ks[19], (DEPTH, N_EXPERTS, D_EXPERT, D_MODEL), D_EXPERT ** -0.5 * BETA),
    }


def reference(x, c, ctx, c_ctx, w_mod, b_mod, w_in, w_fourier, lam_qk, subln_g, w_out,
              ln_attn_g, ln_attn_b, ln_ffn_g, ln_ffn_b, w_router, router_bias,
              w_gate, w_up, w_down):
    B, S, _ = x.shape
    L = ctx.shape[1]
    ROWS = S // GRID_W
    row = jnp.repeat(jnp.arange(ROWS), GRID_W).astype(jnp.float32)
    col = jnp.tile(jnp.arange(GRID_W), ROWS).astype(jnp.float32)
    freqs = ROPE_BASE ** (-jnp.arange(0, ROPE_AXIS_DIM, 2, dtype=jnp.float32) / ROPE_AXIS_DIM)
    ang_row = row[:, None] * freqs
    ang_col = col[:, None] * freqs

    silu_c = jax.nn.silu(c)
    silu_cc = jax.nn.silu(c_ctx)
    xc = ctx

    for l in range(DEPTH):
        last = l == DEPTH - 1
        lam_init = 0.8 - 0.6 * math.exp(-0.3 * l)
        lq = lam_qk[l].astype(jnp.float32)
        lam = jnp.exp(jnp.sum(lq[0] * lq[1])) - jnp.exp(jnp.sum(lq[2] * lq[3])) + lam_init

        mod = silu_c @ w_mod[l] + b_mod[l]
        sh_a, sc_a, g_a, sh_f, sc_f, g_f = [m[:, None, :] for m in jnp.split(mod, 6, axis=-1)]
        mod_c = silu_cc @ w_mod[l] + b_mod[l]
        csh_a, csc_a, cg_a, csh_f, csc_f, cg_f = jnp.split(mod_c, 6)

        h = modulate(layer_norm(x), sh_a, sc_a)
        hc = modulate(layer_norm(xc), csh_a, csc_a)
        p = h @ w_in[l]
        u = p[..., :Q_OFF]
        q = rope2d(p[..., Q_OFF:K_OFF].reshape(B, S, DIFF_SUBHEADS, DIFF_HEAD_DIM), ang_row, ang_col)
        k = rope2d(p[..., K_OFF:V_OFF].reshape(B, S, DIFF_SUBHEADS, DIFF_HEAD_DIM), ang_row, ang_col)
        v = p[..., V_OFF:].reshape(B, S, DIFF_HEADS, DIFF_V_DIM)
        if last:
            pkv = hc @ w_in[l][:, K_OFF:]
            kc = pkv[..., :ATTN_WIDTH].reshape(B, L, DIFF_SUBHEADS, DIFF_HEAD_DIM)
            vc = pkv[..., ATTN_WIDTH:].reshape(B, L, DIFF_HEADS, DIFF_V_DIM)
        else:
            pc = hc @ w_in[l]
            uc = pc[..., :Q_OFF]
            qc = pc[..., Q_OFF:K_OFF].reshape(B, L, DIFF_SUBHEADS, DIFF_HEAD_DIM)
            kc = pc[..., K_OFF:V_OFF].reshape(B, L, DIFF_SUBHEADS, DIFF_HEAD_DIM)
            vc = pc[..., V_OFF:].reshape(B, L, DIFF_HEADS, DIFF_V_DIM)

        o_attn = head_norm(diff_attn_latent(q, k, v, kc, vc, lam), subln_g[l], lam_init)
        o = jnp.concatenate([fourier_mix(u, w_fourier[l]), o_attn], axis=-1) @ w_out[l]
        x_new = layer_norm(ALPHA * x + g_a * o, ln_attn_g[l], ln_attn_b[l])

        if not last:
            oc_attn = head_norm(diff_attn_context(qc, kc, vc, lam), subln_g[l], lam_init)
            oc = jnp.concatenate([fourier_mix(uc, w_fourier[l]), oc_attn], axis=-1) @ w_out[l]
            xc = layer_norm(ALPHA * xc + cg_a * oc, ln_attn_g[l], ln_attn_b[l])
        x = x_new

        h = modulate(layer_norm(x), sh_f, sc_f)
        if last:
            y = moe_ffn(h, w_router, router_bias, w_gate[l], w_up[l], w_down[l])
        else:
            hc = modulate(layer_norm(xc), csh_f, csc_f)
            y_all = moe_ffn(jnp.concatenate([hc, h], axis=1), w_router, router_bias,
                            w_gate[l], w_up[l], w_down[l])
            yc, y = y_all[:, :L], y_all[:, L:]
            xc = layer_norm(ALPHA * xc + cg_f * yc, ln_ffn_g[l], ln_ffn_b[l])
        x = layer_norm(ALPHA * x + g_f * y, ln_ffn_g[l], ln_ffn_b[l])

    return x
```

```python
import functools
import math

import jax
import jax.numpy as jnp
import numpy as np
from jax import lax
from jax.experimental import pallas as pl
from jax.experimental.pallas import tpu as pltpu

F32 = jnp.float32
BF16 = jnp.bfloat16

GRID_W = 64
FOURIER_WIDTH = 256
FOURIER_GROUPS = 4
FOURIER_GROUP_DIM = FOURIER_WIDTH // FOURIER_GROUPS
HEAD_DIM = 64
V_DIM = 2 * HEAD_DIM
ROPE_BASE = 10000.0
ROPE_AXIS_DIM = HEAD_DIM // 2
N_EXPERTS = 16
N_GROUPS = 4
EXPERTS_PER_GROUP = N_EXPERTS // N_GROUPS
N_PAIR_CLASSES = N_GROUPS * (EXPERTS_PER_GROUP * (EXPERTS_PER_GROUP - 1) // 2)
TOP_K = 2
LN_EPS = 1e-5
LOG2E = math.log2(math.e)

LANES = 128
VMEM_LIMIT = 48 * 1024 * 1024


def _cparams(sem):
    return pltpu.CompilerParams(dimension_semantics=sem, vmem_limit_bytes=VMEM_LIMIT)


def _split_bf16(a):
    hi = a.astype(BF16)
    lo = (a - hi.astype(F32)).astype(BF16)
    return hi, lo


def _dot(a, b):
    return jnp.dot(a, b, preferred_element_type=F32)


def _dot_nt(a, b):
    return lax.dot_general(a, b, (((1,), (1,)), ((), ())), preferred_element_type=F32)


def _dot3(a, w):
    ah, al = _split_bf16(a)
    wh, wl = _split_bf16(w)
    return _dot(ah, wh) + _dot(ah, wl) + _dot(al, wh)


def _ln(x):
    mu = jnp.mean(x, axis=-1, keepdims=True)
    xc = x - mu
    var = jnp.mean(xc * xc, axis=-1, keepdims=True)
    return xc * lax.rsqrt(var + LN_EPS)


def _mod_kernel(c_ref, w_ref, b_ref, o_ref):
    c = c_ref[...]
    s = c * jax.nn.sigmoid(c)
    o_ref[0] = _dot3(s, w_ref[0]) + b_ref[0]


def _mod_call(cc, w_mod, b_mod):
    depth, d, n = w_mod.shape
    r = cc.shape[0]
    tn = 1536
    return pl.pallas_call(
        _mod_kernel,
        out_shape=jax.ShapeDtypeStruct((depth, r, n), F32),
        grid=(depth, n // tn),
        in_specs=[
            pl.BlockSpec((r, d), lambda l, j: (0, 0)),
            pl.BlockSpec((1, d, tn), lambda l, j: (l, 0, j)),
            pl.BlockSpec((1, 1, tn), lambda l, j: (l, 0, j)),
        ],
        out_specs=pl.BlockSpec((1, r, tn), lambda l, j: (l, 0, j)),
        compiler_params=_cparams(("arbitrary", "arbitrary")),
        name="mod",
    )(cc, w_mod, b_mod.reshape(depth, 1, n))


def _fprep_kernel(c_ref, s_ref, w_ref, a_ref, b_ref):
    w = w_ref[0]
    a_ref[0] = _dot3(c_ref[...], w)
    b_ref[0] = _dot3(s_ref[...], w)


def _fprep_call(w_fourier):
    depth, g, c, _ = w_fourier.shape
    idx = np.arange(c)
    ang = 2.0 * np.pi * ((idx[:, None] * idx[None, :]) % c) / c
    c64 = jnp.asarray(np.cos(ang), F32)
    s64 = jnp.asarray(np.sin(ang), F32)
    wf = w_fourier.reshape(depth * g, c, c)
    spec = pl.BlockSpec((1, c, c), lambda i: (i, 0, 0))
    cst = pl.BlockSpec((c, c), lambda i: (0, 0))
    a, b = pl.pallas_call(
        _fprep_kernel,
        out_shape=(jax.ShapeDtypeStruct(wf.shape, F32),) * 2,
        grid=(depth * g,),
        in_specs=[cst, cst, spec],
        out_specs=(spec, spec),
        compiler_params=_cparams(("arbitrary",)),
        name="fourier_prep",
    )(c64, s64, wf)
    eye = jnp.eye(g, dtype=F32)

    def bd(m):
        m = m.reshape(depth, g, c, c)
        return (m[:, :, :, None, :] * eye[None, :, None, :, None]).reshape(depth, g * c, g * c)

    return jnp.concatenate([bd(a), bd(b)], axis=-1).astype(BF16)


def _inproj_kernel(*refs, fuse_prev, alpha):
    if fuse_prev:
        (x_ref, y_ref, gf_ref, lg_ref, lb_ref, sh_ref, sc_ref, w_ref, ab_ref,
         cq_ref, sq_ref, ck_ref, sk_ref, u_ref, q_ref, k_ref, v_ref, xo_ref) = refs
        z = alpha * x_ref[0] + gf_ref[0] * y_ref[0].astype(F32)
        x = _ln(z) * lg_ref[...] + lb_ref[...]
        xo_ref[0] = x
    else:
        (x_ref, sh_ref, sc_ref, w_ref, ab_ref,
         cq_ref, sq_ref, ck_ref, sk_ref, u_ref, q_ref, k_ref, v_ref) = refs
        x = x_ref[0]
    h = (_ln(x) * (1.0 + sc_ref[0]) + sh_ref[0]).astype(BF16)

    fw = FOURIER_WIDTH
    aw = (w_ref.shape[1] - fw) // 3
    u = _dot(h, w_ref[:, :fw]).astype(BF16)
    u_ref[0] = _dot(u, ab_ref[...]).astype(BF16)

    tm = h.shape[0]
    lane = lax.broadcasted_iota(jnp.int32, (tm, LANES), 1)
    first_half = (lane & (ROPE_AXIS_DIM // 2)) == 0

    def rope(p, cos, sin):
        outs = []
        for j in range(aw // LANES):
            pj = p[:, j * LANES:(j + 1) * LANES]
            half = ROPE_AXIS_DIM // 2
            swapped = jnp.where(first_half, pltpu.roll(pj, LANES - half, 1), pltpu.roll(pj, half, 1))
            outs.append(pj * cos + swapped * sin)
        return jnp.concatenate(outs, axis=1)

    q = _dot(h, w_ref[:, fw:fw + aw])
    q_ref[0] = rope(q, cq_ref[...], sq_ref[...]).astype(BF16)
    k = _dot(h, w_ref[:, fw + aw:fw + 2 * aw])
    k_ref[0] = rope(k, ck_ref[...], sk_ref[...]).astype(BF16)
    v_ref[0] = _dot(h, w_ref[:, fw + 2 * aw:]).astype(BF16)


def _inproj_call(x, shift, scale, w_in, ab, tabs, prev=None, alpha=1.0, tm=256):
    bsz, t, d = x.shape
    n_in = w_in.shape[1]
    aw = (n_in - FOURIER_WIDTH) // 3
    tm = min(tm, t)
    grid = (bsz, t // tm)
    row = pl.BlockSpec((1, tm, d), lambda b, i: (b, i, 0))
    vec = pl.BlockSpec((1, 1, d), lambda b, i: (b, 0, 0))
    cst = lambda shp: pl.BlockSpec(shp, lambda b, i: (0,) * len(shp))
    tab = pl.BlockSpec((tm, LANES), lambda b, i: (i, 0))
    in_specs, args = [row], [x]
    if prev is not None:
        y, gf, lg, lb = prev
        in_specs += [row, vec, cst((1, d)), cst((1, d))]
        args += [y, gf, lg.reshape(1, d), lb.reshape(1, d)]
    in_specs += [vec, vec, cst((d, n_in)), cst(ab.shape), tab, tab, tab, tab]
    args += [shift, scale, w_in, ab, *tabs]
    out_shape = [
        jax.ShapeDtypeStruct((bsz, t, 2 * FOURIER_WIDTH), BF16),
        jax.ShapeDtypeStruct((bsz, t, aw), BF16),
        jax.ShapeDtypeStruct((bsz, t, aw), BF16),
        jax.ShapeDtypeStruct((bsz, t, aw), BF16),
    ]
    out_specs = [
        pl.BlockSpec((1, tm, 2 * FOURIER_WIDTH), lambda b, i: (b, i, 0)),
        pl.BlockSpec((1, tm, aw), lambda b, i: (b, i, 0)),
        pl.BlockSpec((1, tm, aw), lambda b, i: (b, i, 0)),
        pl.BlockSpec((1, tm, aw), lambda b, i: (b, i, 0)),
    ]
    if prev is not None:
        out_shape.append(jax.ShapeDtypeStruct((bsz, t, d), F32))
        out_specs.append(row)
    return pl.pallas_call(
        functools.partial(_inproj_kernel, fuse_prev=prev is not None, alpha=alpha),
        out_shape=tuple(out_shape),
        grid=grid,
        in_specs=in_specs,
        out_specs=tuple(out_specs),
        compiler_params=_cparams(("arbitrary", "arbitrary")),
        name="inproj",
    )(*args)


def _attn_kernel(lam_ref, q_ref, *refs, n_seg, norm_scale):
    kv_refs = refs[:2 * n_seg]
    g_ref, o_ref = refs[2 * n_seg], refs[2 * n_seg + 1]
    qb = q_ref[0]
    tq = qb.shape[0]
    lane = lax.broadcasted_iota(jnp.int32, (tq, LANES), 1)
    zero = jnp.zeros_like(qb)
    heads = []
    for sub in range(2):
        qs = jnp.where((lane < HEAD_DIM) == (sub == 0), qb, zero)
        scores = [_dot_nt(qs, kv_refs[2 * j][0]) for j in range(n_seg)]
        m = functools.reduce(jnp.maximum, [jnp.max(s, axis=-1, keepdims=True) for s in scores])
        acc = None
        den = None
        for j in range(n_seg):
            p = jnp.exp2(scores[j] - m)
            dj = jnp.sum(p, axis=-1, keepdims=True)
            aj = _dot(p.astype(BF16), kv_refs[2 * j + 1][0])
            acc = aj if acc is None else acc + aj
            den = dj if den is None else den + dj
        heads.append(acc / den)
    o = heads[0] - lam_ref[0] * heads[1]
    o = o * lax.rsqrt(jnp.mean(o * o, axis=-1, keepdims=True) + LN_EPS)
    o_ref[0] = (o * g_ref[...] * norm_scale).astype(BF16)


def _attn_call(q, kv_segs, lam, g, lam_init, tq=256):
    bsz, t, aw = q.shape
    nh = aw // V_DIM
    tq = min(tq, t)
    grid = (bsz, nh, t // tq)
    in_specs = [
        pl.BlockSpec(memory_space=pltpu.SMEM),
        pl.BlockSpec((1, tq, V_DIM), lambda b, h, i: (b, i, h)),
    ]
    args = [lam.reshape(1).astype(F32), q]
    for k, v in kv_segs:
        tk = k.shape[1]
        in_specs += [pl.BlockSpec((1, tk, V_DIM), lambda b, h, i: (b, 0, h))] * 2
        args += [k, v]
    in_specs.append(pl.BlockSpec((1, V_DIM), lambda b, h, i: (0, 0)))
    args.append(g.reshape(1, V_DIM).astype(F32))
    return pl.pallas_call(
        functools.partial(_attn_kernel, n_seg=len(kv_segs), norm_scale=1.0 - lam_init),
        out_shape=jax.ShapeDtypeStruct((bsz, t, aw), BF16),
        grid=grid,
        in_specs=in_specs,
        out_specs=pl.BlockSpec((1, tq, V_DIM), lambda b, h, i: (b, i, h)),
        compiler_params=_cparams(("arbitrary",) * 3),
        name="diff_attn",
    )(*args)


def _fourier_kernel(c_ref, s_ref, u1_ref, u2_ref, o_ref):
    o_ref[0] = (_dot(c_ref[...], u1_ref[0]) - _dot(s_ref[...], u2_ref[0])).astype(BF16)


def _dft_mats(t):
    idx = jnp.arange(t, dtype=jnp.int32)
    ang = ((idx[:, None] * idx[None, :]) % t).astype(F32) * (2.0 * math.pi / t)
    scale = 1.0 / math.sqrt(t * FOURIER_GROUP_DIM)
    return (jnp.cos(ang) * scale).astype(BF16), (jnp.sin(ang) * scale).astype(BF16)


def _fourier_call(u12, cmat, smat, tk=1024):
    bsz, t, w2 = u12.shape
    w = w2 // 2
    tk = min(tk, t)
    return pl.pallas_call(
        _fourier_kernel,
        out_shape=jax.ShapeDtypeStruct((bsz, t, w), BF16),
        grid=(t // tk, bsz),
        in_specs=[
            pl.BlockSpec((tk, t), lambda i, b: (i, 0)),
            pl.BlockSpec((tk, t), lambda i, b: (i, 0)),
            pl.BlockSpec((1, t, w), lambda i, b: (b, 0, 0)),
            pl.BlockSpec((1, t, w), lambda i, b: (b, 0, 1)),
        ],
        out_specs=pl.BlockSpec((1, tk, w), lambda i, b: (b, i, 0)),
        compiler_params=_cparams(("arbitrary", "arbitrary")),
        name="fourier_dft",
    )(cmat, smat, u12, u12)


def _outproj_kernel(f_ref, o_ref, x_ref, ga_ref, lg_ref, lb_ref, sh_ref, sc_ref, w_ref, wr_ref,
                    xo_ref, h_ref, lg_out_ref, *, alpha):
    fw = f_ref.shape[2]
    proj = _dot(f_ref[0], w_ref[:fw, :]) + _dot(o_ref[0], w_ref[fw:, :])
    z = alpha * x_ref[0] + ga_ref[0] * proj
    x = _ln(z) * lg_ref[...] + lb_ref[...]
    xo_ref[0] = x
    h = _ln(x) * (1.0 + sc_ref[0]) + sh_ref[0]
    hh, hl = _split_bf16(h)
    h_ref[0] = hh
    r = _dot_nt(wr_ref[...], hh)
    r2 = _dot_nt(wr_ref[:N_EXPERTS, :], hl)
    lg_out_ref[...] = r[:N_EXPERTS] + r[N_EXPERTS:] + r2


def _outproj_call(f, o, x, g_a, ln_g, ln_b, shift, scale, w_out, wr_hilo, alpha, tm=256):
    bsz, t, d = x.shape
    tm = min(tm, t)
    nt = t // tm
    row = lambda w: pl.BlockSpec((1, tm, w), lambda b, i: (b, i, 0))
    vec = pl.BlockSpec((1, 1, d), lambda b, i: (b, 0, 0))
    cst = lambda shp: pl.BlockSpec(shp, lambda b, i: (0,) * len(shp))
    return pl.pallas_call(
        functools.partial(_outproj_kernel, alpha=alpha),
        out_shape=(
            jax.ShapeDtypeStruct((bsz, t, d), F32),
            jax.ShapeDtypeStruct((bsz, t, d), BF16),
            jax.ShapeDtypeStruct((N_EXPERTS, bsz * t), F32),
        ),
        grid=(bsz, nt),
        in_specs=[row(f.shape[2]), row(o.shape[2]), row(d), vec, cst((1, d)), cst((1, d)), vec, vec,
                  cst(w_out.shape), cst(wr_hilo.shape)],
        out_specs=(row(d), row(d), pl.BlockSpec((N_EXPERTS, tm), lambda b, i: (0, b * nt + i))),
        compiler_params=_cparams(("arbitrary", "arbitrary")),
        name="outproj",
    )(f, o, x, g_a, ln_g.reshape(1, d), ln_b.reshape(1, d), shift, scale, w_out, wr_hilo)


def _moe_kernel(e0_ref, e1_ref, valid_ref, x_ref, wt_ref, wg0, wu0, wd0, wg1, wu1, wd1, y_ref):
    i = pl.program_id(0)

    @pl.when(valid_ref[i] != 0)
    def _():
        x = x_ref[...]

        def expert(wg, wu, wd):
            g = _dot(x, wg[0].astype(BF16))
            u = _dot(x, wu[0].astype(BF16))
            a = (g * jax.nn.sigmoid(g) * u).astype(BF16)
            return _dot(a, wd[0].astype(BF16))

        wt = wt_ref[...]
        y = wt[:, 0:1] * expert(wg0, wu0, wd0) + wt[:, 1:2] * expert(wg1, wu1, wd1)
        y_ref[...] = y.astype(BF16)

    @pl.when(valid_ref[i] == 0)
    def _():
        y_ref[...] = jnp.zeros_like(y_ref)


def _moe_call(xs, wts, tile_e0, tile_e1, tile_valid, w_gate, w_up, w_down, tmm):
    tpad, d = xs.shape
    de = w_gate.shape[2]
    n_tiles = tpad // tmm
    wspec0 = lambda shp: pl.BlockSpec((1,) + shp, lambda i, e0, e1, vl: (e0[i], 0, 0))
    wspec1 = lambda shp: pl.BlockSpec((1,) + shp, lambda i, e0, e1, vl: (e1[i], 0, 0))
    grid_spec = pltpu.PrefetchScalarGridSpec(
        num_scalar_prefetch=3,
        grid=(n_tiles,),
        in_specs=[
            pl.BlockSpec((tmm, d), lambda i, e0, e1, vl: (i, 0)),
            pl.BlockSpec((tmm, wts.shape[1]), lambda i, e0, e1, vl: (i, 0)),
            wspec0((d, de)), wspec0((d, de)), wspec0((de, d)),
            wspec1((d, de)), wspec1((d, de)), wspec1((de, d)),
        ],
        out_specs=pl.BlockSpec((tmm, d), lambda i, e0, e1, vl: (i, 0)),
    )
    return pl.pallas_call(
        _moe_kernel,
        out_shape=jax.ShapeDtypeStruct((tpad, d), BF16),
        grid_spec=grid_spec,
        compiler_params=_cparams(("arbitrary",)),
        name="moe_ffn",
    )(tile_e0, tile_e1, tile_valid, xs, wts, w_gate, w_up, w_down, w_gate, w_up, w_down)


def _route(logits_t, router_bias):
    logits = logits_t.T
    scores = jax.nn.sigmoid(logits)
    sel = scores + router_bias.astype(F32)
    grp_sel = sel.reshape(sel.shape[0], N_GROUPS, EXPERTS_PER_GROUP)
    grp_score = lax.top_k(grp_sel, TOP_K)[0].sum(-1)
    best_grp = jnp.argmax(grp_score, axis=-1)
    in_grp = (jnp.arange(N_EXPERTS) // EXPERTS_PER_GROUP) == best_grp[:, None]
    masked = jnp.where(in_grp, sel, -jnp.inf)
    _, idx = lax.top_k(masked, TOP_K)
    w = jnp.take_along_axis(scores, idx, axis=-1)
    w = w / jnp.sum(w, axis=-1, keepdims=True)
    swap = idx[:, 0] > idx[:, 1]
    e0 = jnp.where(swap, idx[:, 1], idx[:, 0]).astype(jnp.int32)
    e1 = jnp.where(swap, idx[:, 0], idx[:, 1]).astype(jnp.int32)
    w0 = jnp.where(swap, w[:, 1], w[:, 0])
    w1 = jnp.where(swap, w[:, 0], w[:, 1])
    return e0, e1, w0, w1


def _pair_tables():
    e0s, e1s = [], []
    for g in range(N_GROUPS):
        for a in range(EXPERTS_PER_GROUP):
            for b in range(a + 1, EXPERTS_PER_GROUP):
                e0s.append(g * EXPERTS_PER_GROUP + a)
                e1s.append(g * EXPERTS_PER_GROUP + b)
    return np.asarray(e0s, np.int32), np.asarray(e1s, np.int32)


def _moe_ffn(h2, logits_t, router_bias, w_gate, w_up, w_down, tmm=256):
    t, d = h2.shape
    e0, e1, w0, w1 = _route(logits_t, router_bias)
    a = e0 % EXPERTS_PER_GROUP
    b = e1 % EXPERTS_PER_GROUP
    pairs_per_group = N_PAIR_CLASSES // N_GROUPS
    cls = (e0 // EXPERTS_PER_GROUP) * pairs_per_group + a * (2 * EXPERTS_PER_GROUP - 1 - a) // 2 + (b - a - 1)
    onehot = (cls[:, None] == jnp.arange(N_PAIR_CLASSES, dtype=jnp.int32)[None, :]).astype(jnp.int32)
    rank = jnp.take_along_axis(jnp.cumsum(onehot, axis=0) - onehot, cls[:, None], axis=1)[:, 0]
    counts = jnp.sum(onehot, axis=0)
    padded = ((counts + tmm - 1) // tmm) * tmm
    ends = jnp.cumsum(padded)
    starts = ends - padded
    pos = starts[cls] + rank
    n_tiles = t // tmm + N_PAIR_CLASSES
    tpad = n_tiles * tmm
    total = ends[-1]
    tile_start = jnp.arange(n_tiles, dtype=jnp.int32) * tmm
    tile_valid = (tile_start < total).astype(jnp.int32)
    tile_cls = jnp.searchsorted(ends, jnp.minimum(tile_start, total - 1), side="right").astype(jnp.int32)
    tile_cls = jnp.minimum(tile_cls, N_PAIR_CLASSES - 1)
    pe0, pe1 = _pair_tables()
    tile_e0 = jnp.asarray(pe0)[tile_cls]
    tile_e1 = jnp.asarray(pe1)[tile_cls]
    tok_of_pos = jnp.zeros((tpad,), jnp.int32).at[pos].set(jnp.arange(t, dtype=jnp.int32))
    xs = jnp.take(h2, tok_of_pos, axis=0)
    wts = jnp.zeros((tpad, 8), F32).at[pos, 0].set(w0).at[pos, 1].set(w1)
    ys = _moe_call(xs, wts, tile_e0, tile_e1, tile_valid, w_gate, w_up, w_down, tmm)
    return jnp.take(ys, pos, axis=0)


def _ffn_ln_kernel(x_ref, y_ref, gf_ref, lg_ref, lb_ref, o_ref, *, alpha):
    z = alpha * x_ref[0] + gf_ref[0] * y_ref[0].astype(F32)
    o_ref[0] = _ln(z) * lg_ref[...] + lb_ref[...]


def _ffn_ln_call(x, y, g_f, ln_g, ln_b, alpha, tm=512):
    bsz, t, d = x.shape
    tm = min(tm, t)
    row = pl.BlockSpec((1, tm, d), lambda b, i: (b, i, 0))
    vec = pl.BlockSpec((1, 1, d), lambda b, i: (b, 0, 0))
    cst = pl.BlockSpec((1, d), lambda b, i: (0, 0))
    return pl.pallas_call(
        functools.partial(_ffn_ln_kernel, alpha=alpha),
        out_shape=jax.ShapeDtypeStruct((bsz, t, d), F32),
        grid=(bsz, t // tm),
        in_specs=[row, row, vec, cst, cst],
        out_specs=row,
        compiler_params=_cparams(("arbitrary", "arbitrary")),
        name="ffn_ln",
    )(x, y, g_f, ln_g.reshape(1, d), ln_b.reshape(1, d))


def _rope_tables(s):
    rows = s // GRID_W
    row = jnp.repeat(jnp.arange(rows), GRID_W).astype(F32)
    col = jnp.tile(jnp.arange(GRID_W), rows).astype(F32)
    freqs = ROPE_BASE ** (-jnp.arange(0, ROPE_AXIS_DIM, 2, dtype=F32) / ROPE_AXIS_DIM)
    ang_row = row[:, None] * freqs
    ang_col = col[:, None] * freqs

    def head(r, c, sign):
        return jnp.concatenate([sign * r, r, sign * c, c], axis=1)

    cos = head(jnp.cos(ang_row), jnp.cos(ang_col), 1.0)
    sin = head(jnp.sin(ang_row), jnp.sin(ang_col), -1.0)
    cos = jnp.concatenate([cos, cos], axis=1)
    sin = jnp.concatenate([sin, sin], axis=1)
    return cos, sin


def kernel(x, c, ctx, c_ctx, w_mod, b_mod, w_in, w_fourier, lam_qk, subln_g, w_out, ln_attn_g, ln_attn_b,
           ln_ffn_g, ln_ffn_b, w_router, router_bias, w_gate, w_up, w_down):
    bsz, s, d = x.shape
    l_ctx = ctx.shape[1]
    depth = w_mod.shape[0]
    alpha = (2 * depth) ** 0.25
    qscale = LOG2E * HEAD_DIM ** -0.5

    pad = (-(bsz + 1)) % 8
    cc = jnp.concatenate([c, c_ctx[None, :], jnp.zeros((pad, d), F32)], axis=0)
    mod = _mod_call(cc, w_mod, b_mod)

    ab = _fprep_call(w_fourier)
    w_in_b = w_in.astype(BF16)
    w_out_b = w_out.astype(BF16)
    wr_t = w_router.T.astype(F32)
    wr_hi = wr_t.astype(BF16)
    wr_lo = (wr_t - wr_hi.astype(F32)).astype(BF16)
    wr_hilo = jnp.concatenate([wr_hi, wr_lo], axis=0)

    cos, sin = _rope_tables(s)
    tabs_lat = (cos * qscale, sin * qscale, cos, sin)
    ones = jnp.ones((l_ctx, LANES), F32)
    zeros = jnp.zeros((l_ctx, LANES), F32)
    tabs_ctx = (ones * qscale, zeros, ones, zeros)
    dft_lat = _dft_mats(s)
    dft_ctx = _dft_mats(l_ctx)

    xc = ctx
    prev = None
    prev_c = None
    for l in range(depth):
        last = l == depth - 1
        lam_init = 0.8 - 0.6 * math.exp(-0.3 * l)
        lq = lam_qk[l].astype(F32)
        lam = jnp.exp(jnp.sum(lq[0] * lq[1])) - jnp.exp(jnp.sum(lq[2] * lq[3])) + lam_init

        m_lat = mod[l, :bsz].reshape(bsz, 1, 6, d)
        sh_a, sc_a, g_a, sh_f, sc_f, g_f = [m_lat[:, :, i, :] for i in range(6)]
        m_ctx = jnp.broadcast_to(mod[l, bsz].reshape(1, 1, 6, d), (bsz, 1, 6, d))
        csh_a, csc_a, cg_a, csh_f, csc_f, cg_f = [m_ctx[:, :, i, :] for i in range(6)]

        outs = _inproj_call(x, sh_a, sc_a, w_in_b[l], ab[l], tabs_lat, prev=prev, alpha=alpha)
        u12, q, k, v = outs[:4]
        if prev is not None:
            x = outs[4]
        outs_c = _inproj_call(xc, csh_a, csc_a, w_in_b[l], ab[l], tabs_ctx, prev=prev_c, alpha=alpha)
        u12c, qc, kc, vc = outs_c[:4]
        if prev_c is not None:
            xc = outs_c[4]

        o_attn = _attn_call(q, [(kc, vc), (k, v)], lam, subln_g[l], lam_init)
        f = _fourier_call(u12, *dft_lat)
        x, h2, logits_t = _outproj_call(f, o_attn, x, g_a, ln_attn_g[l], ln_attn_b[l], sh_f, sc_f,
                                        w_out_b[l], wr_hilo, alpha)
        h2 = h2.reshape(bsz * s, d)

        if not last:
            oc_attn = _attn_call(qc, [(kc, vc)], lam, subln_g[l], lam_init)
            fc = _fourier_call(u12c, *dft_ctx)
            xc, h2c, logits_c = _outproj_call(fc, oc_attn, xc, cg_a, ln_attn_g[l], ln_attn_b[l], csh_f, csc_f,
                                              w_out_b[l], wr_hilo, alpha)
            h2_all = jnp.concatenate([h2c.reshape(bsz * l_ctx, d), h2], axis=0)
            logits_all = jnp.concatenate([logits_c, logits_t], axis=1)
            y_all = _moe_ffn(h2_all, logits_all, router_bias, w_gate[l], w_up[l], w_down[l])
            yc = y_all[:bsz * l_ctx].reshape(bsz, l_ctx, d)
            y = y_all[bsz * l_ctx:].reshape(bsz, s, d)
            prev_c = (yc, cg_f, ln_ffn_g[l], ln_ffn_b[l])
        else:
            y = _moe_ffn(h2, logits_t, router_bias, w_gate[l], w_up[l], w_down[l]).reshape(bsz, s, d)
        prev = (y, g_f, ln_ffn_g[l], ln_ffn_b[l])

    y, g_f, lg, lb = prev
    return _ffn_ln_call(x, y, g_f, lg, lb, alpha)
```

```python
import functools
import math

import jax
import jax.numpy as jnp
import numpy as np
from jax import lax
from jax.experimental import pallas as pl
from jax.experimental.pallas import tpu as pltpu

F32 = jnp.float32
BF16 = jnp.bfloat16

GRID_W = 64
FOURIER_WIDTH = 256
FOURIER_GROUPS = 4
FOURIER_GROUP_DIM = FOURIER_WIDTH // FOURIER_GROUPS
HEAD_DIM = 64
V_DIM = 2 * HEAD_DIM
ROPE_BASE = 10000.0
ROPE_AXIS_DIM = HEAD_DIM // 2
N_EXPERTS = 16
N_GROUPS = 4
EXPERTS_PER_GROUP = N_EXPERTS // N_GROUPS
N_PAIR_CLASSES = N_GROUPS * (EXPERTS_PER_GROUP * (EXPERTS_PER_GROUP - 1) // 2)
TOP_K = 2
LN_EPS = 1e-5
LOG2E = math.log2(math.e)

LANES = 128
SUBLANES_BF16 = 16
VMEM_LIMIT = 48 * 1024 * 1024


def _cparams(sem):
    return pltpu.CompilerParams(dimension_semantics=sem, vmem_limit_bytes=VMEM_LIMIT)


def _split_bf16(a):
    hi = a.astype(BF16)
    lo = (a - hi.astype(F32)).astype(BF16)
    return hi, lo


def _dot(a, b):
    return jnp.dot(a, b, preferred_element_type=F32)


def _dot_nt(a, b):
    return lax.dot_general(a, b, (((1,), (1,)), ((), ())), preferred_element_type=F32)


def _dot3(a, w):
    ah, al = _split_bf16(a)
    wh, wl = _split_bf16(w)
    return _dot(ah, wh) + _dot(ah, wl) + _dot(al, wh)


def _ln(x):
    mu = jnp.mean(x, axis=-1, keepdims=True)
    xc = x - mu
    var = jnp.mean(xc * xc, axis=-1, keepdims=True)
    return xc * lax.rsqrt(var + LN_EPS)


def _mod_kernel(c_ref, w_ref, b_ref, o_ref):
    c = c_ref[...]
    s = c * jax.nn.sigmoid(c)
    o_ref[0] = _dot3(s, w_ref[0]) + b_ref[0]


def _mod_call(cc, w_mod, b_mod):
    depth, d, n = w_mod.shape
    r = cc.shape[0]
    tn = 1536
    return pl.pallas_call(
        _mod_kernel,
        out_shape=jax.ShapeDtypeStruct((depth, r, n), F32),
        grid=(depth, n // tn),
        in_specs=[
            pl.BlockSpec((r, d), lambda l, j: (0, 0)),
            pl.BlockSpec((1, d, tn), lambda l, j: (l, 0, j)),
            pl.BlockSpec((1, 1, tn), lambda l, j: (l, 0, j)),
        ],
        out_specs=pl.BlockSpec((1, r, tn), lambda l, j: (l, 0, j)),
        compiler_params=_cparams(("arbitrary", "arbitrary")),
        name="mod",
    )(cc, w_mod, b_mod.reshape(depth, 1, n))


def _fprep_kernel(c_ref, s_ref, w_ref, a_ref, b_ref):
    w = w_ref[0]
    a_ref[0] = _dot3(c_ref[...], w)
    b_ref[0] = _dot3(s_ref[...], w)


def _fprep_call(w_fourier):
    depth, g, c, _ = w_fourier.shape
    idx = np.arange(c)
    ang = 2.0 * np.pi * ((idx[:, None] * idx[None, :]) % c) / c
    c64 = jnp.asarray(np.cos(ang), F32)
    s64 = jnp.asarray(np.sin(ang), F32)
    wf = w_fourier.reshape(depth * g, c, c)
    spec = pl.BlockSpec((1, c, c), lambda i: (i, 0, 0))
    cst = pl.BlockSpec((c, c), lambda i: (0, 0))
    a, b = pl.pallas_call(
        _fprep_kernel,
        out_shape=(jax.ShapeDtypeStruct(wf.shape, F32),) * 2,
        grid=(depth * g,),
        in_specs=[cst, cst, spec],
        out_specs=(spec, spec),
        compiler_params=_cparams(("arbitrary",)),
        name="fourier_prep",
    )(c64, s64, wf)
    eye = jnp.eye(g, dtype=F32)

    def bd(m):
        m = m.reshape(depth, g, c, c)
        return (m[:, :, :, None, :] * eye[None, :, None, :, None]).reshape(depth, g * c, g * c)

    return jnp.concatenate([bd(a), bd(b)], axis=-1).astype(BF16)


def _inproj_kernel(*refs, fuse_prev, alpha):
    if fuse_prev:
        (x_ref, y_ref, gf_ref, lg_ref, lb_ref, sh_ref, sc_ref, w_ref, wvt_ref, ab_ref,
         cq_ref, sq_ref, ck_ref, sk_ref, u_ref, q_ref, k_ref, v_ref, xo_ref) = refs
        z = alpha * x_ref[0] + gf_ref[0] * y_ref[0].astype(F32)
        x = _ln(z) * lg_ref[...] + lb_ref[...]
        xo_ref[0] = x
    else:
        (x_ref, sh_ref, sc_ref, w_ref, wvt_ref, ab_ref,
         cq_ref, sq_ref, ck_ref, sk_ref, u_ref, q_ref, k_ref, v_ref) = refs
        x = x_ref[0]
    h = (_ln(x) * (1.0 + sc_ref[0]) + sh_ref[0]).astype(BF16)

    fw = FOURIER_WIDTH
    aw = (w_ref.shape[1] - fw) // 2
    nh = aw // LANES
    u = _dot(h, w_ref[:, :fw]).astype(BF16)
    u_ref[0] = _dot(u, ab_ref[...]).astype(BF16)

    tm = h.shape[0]
    lane = lax.broadcasted_iota(jnp.int32, (tm, LANES), 1)
    half = ROPE_AXIS_DIM // 2
    first_half = (lane & half) == 0

    def rope_store(p, cos, sin, dst_ref):
        for j in range(nh):
            pj = p[:, j * LANES:(j + 1) * LANES]
            swapped = jnp.where(first_half, pltpu.roll(pj, LANES - half, 1), pltpu.roll(pj, half, 1))
            dst_ref[0, j] = (pj * cos + swapped * sin).astype(BF16)

    rope_store(_dot(h, w_ref[:, fw:fw + aw]), cq_ref[...], sq_ref[...], q_ref)
    rope_store(_dot(h, w_ref[:, fw + aw:]), ck_ref[...], sk_ref[...], k_ref)
    vt = _dot_nt(wvt_ref[...], h)
    for j in range(nh):
        v_ref[0, j] = vt[j * V_DIM:(j + 1) * V_DIM, :].astype(BF16)


def _inproj_call(x, shift, scale, w_uqk, w_vt, ab, tabs, prev=None, alpha=1.0, tm=256):
    bsz, t, d = x.shape
    aw = w_vt.shape[0]
    nh = aw // V_DIM
    tm = min(tm, t)
    grid = (bsz, t // tm)
    row = pl.BlockSpec((1, tm, d), lambda b, i: (b, i, 0))
    vec = pl.BlockSpec((1, 1, d), lambda b, i: (b, 0, 0))
    cst = lambda shp: pl.BlockSpec(shp, lambda b, i: (0,) * len(shp))
    tab = pl.BlockSpec((tm, LANES), lambda b, i: (i, 0))
    in_specs, args = [row], [x]
    if prev is not None:
        y, gf, lg, lb = prev
        in_specs += [row, vec, cst((1, d)), cst((1, d))]
        args += [y, gf, lg.reshape(1, d), lb.reshape(1, d)]
    in_specs += [vec, vec, cst(w_uqk.shape), cst(w_vt.shape), cst(ab.shape), tab, tab, tab, tab]
    args += [shift, scale, w_uqk, w_vt, ab, *tabs]
    head_rows = pl.BlockSpec((1, nh, tm, V_DIM), lambda b, i: (b, 0, i, 0))
    out_shape = [
        jax.ShapeDtypeStruct((bsz, t, 2 * FOURIER_WIDTH), BF16),
        jax.ShapeDtypeStruct((bsz, nh, t, V_DIM), BF16),
        jax.ShapeDtypeStruct((bsz, nh, t, V_DIM), BF16),
        jax.ShapeDtypeStruct((bsz, nh, V_DIM, t), BF16),
    ]
    out_specs = [
        pl.BlockSpec((1, tm, 2 * FOURIER_WIDTH), lambda b, i: (b, i, 0)),
        head_rows,
        head_rows,
        pl.BlockSpec((1, nh, V_DIM, tm), lambda b, i: (b, 0, 0, i)),
    ]
    if prev is not None:
        out_shape.append(jax.ShapeDtypeStruct((bsz, t, d), F32))
        out_specs.append(row)
    return pl.pallas_call(
        functools.partial(_inproj_kernel, fuse_prev=prev is not None, alpha=alpha),
        out_shape=tuple(out_shape),
        grid=grid,
        in_specs=in_specs,
        out_specs=tuple(out_specs),
        compiler_params=_cparams(("arbitrary", "arbitrary")),
        name="inproj",
    )(*args)


def _attn_kernel(lam_ref, q_ref, *refs, n_seg, norm_scale):
    kv_refs = refs[:2 * n_seg]
    g_ref, o_ref, s_buf, m_buf = refs[2 * n_seg:]
    nh, tq = q_ref.shape[1], q_ref.shape[2]
    lane = lax.broadcasted_iota(jnp.int32, (tq, LANES), 1)
    lam = lam_ref[0]
    gain = g_ref[...] * norm_scale
    seg_len = [kv_refs[2 * j].shape[2] for j in range(n_seg)]
    seg_off = [sum(seg_len[:j]) for j in range(n_seg)]

    def scores(h, sub):
        qb = q_ref[0, h]
        qs = jnp.where((lane < HEAD_DIM) == (sub == 0), qb, jnp.zeros_like(qb))
        m = None
        for j in range(n_seg):
            s = _dot_nt(kv_refs[2 * j][0, h], qs)
            s_buf[sub, seg_off[j]:seg_off[j] + seg_len[j], :] = s
            mj = jnp.max(s, axis=0, keepdims=True)
            m = mj if m is None else jnp.maximum(m, mj)
        m_buf[sub] = jnp.broadcast_to(m, m_buf.shape[1:])

    def attend(h, sub):
        m = m_buf[sub][0:1]
        acc = None
        for j in range(n_seg):
            p = jnp.exp2(s_buf[sub, seg_off[j]:seg_off[j] + seg_len[j], :] - m).astype(BF16)
            vt = kv_refs[2 * j + 1][0, h]
            vext = jnp.concatenate([vt, jnp.ones((SUBLANES_BF16, vt.shape[1]), BF16)], axis=0)
            aj = _dot(vext, p)
            acc = aj if acc is None else acc + aj
        return acc[:V_DIM] / acc[V_DIM:V_DIM + 1]

    def finish(h, o0, o1):
        o = o0 - lam * o1
        o = o * lax.rsqrt(jnp.mean(o * o, axis=0, keepdims=True) + LN_EPS)
        o_ref[0, h] = (o.T * gain).astype(BF16)

    scores(0, 0)

    def head(h, carry):
        scores(h, 1)
        o0 = attend(h, 0)
        scores(h + 1, 0)
        o1 = attend(h, 1)
        finish(h, o0, o1)
        return carry

    lax.fori_loop(0, nh - 1, head, 0)
    scores(nh - 1, 1)
    o0 = attend(nh - 1, 0)
    o1 = attend(nh - 1, 1)
    finish(nh - 1, o0, o1)


def _attn_call(q, kv_segs, lam, g, lam_init, tq=512):
    bsz, nh, t, _ = q.shape
    tq = min(tq, t)
    grid = (bsz, t // tq)
    qspec = pl.BlockSpec((1, nh, tq, V_DIM), lambda b, i: (b, 0, i, 0))
    in_specs = [pl.BlockSpec(memory_space=pltpu.SMEM), qspec]
    args = [lam.reshape(1).astype(F32), q]
    for k, vt in kv_segs:
        in_specs += [pl.BlockSpec((1,) + k.shape[1:], lambda b, i: (b, 0, 0, 0)),
                     pl.BlockSpec((1,) + vt.shape[1:], lambda b, i: (b, 0, 0, 0))]
        args += [k, vt]
    in_specs.append(pl.BlockSpec((1, V_DIM), lambda b, i: (0, 0)))
    args.append(g.reshape(1, V_DIM).astype(F32))
    n_keys = sum(k.shape[2] for k, _ in kv_segs)
    return pl.pallas_call(
        functools.partial(_attn_kernel, n_seg=len(kv_segs), norm_scale=1.0 - lam_init),
        out_shape=jax.ShapeDtypeStruct(q.shape, BF16),
        grid=grid,
        in_specs=in_specs,
        out_specs=qspec,
        scratch_shapes=[pltpu.VMEM((2, n_keys, tq), F32), pltpu.VMEM((2, 8, tq), F32)],
        compiler_params=_cparams(("arbitrary", "arbitrary")),
        name="diff_attn",
    )(*args)


def _fourier_kernel(c_ref, s_ref, u1_ref, u2_ref, o_ref):
    o_ref[0] = (_dot(c_ref[...], u1_ref[0]) - _dot(s_ref[...], u2_ref[0])).astype(BF16)


def _dft_mats(t):
    idx = jnp.arange(t, dtype=jnp.int32)
    ang = ((idx[:, None] * idx[None, :]) % t).astype(F32) * (2.0 * math.pi / t)
    scale = 1.0 / math.sqrt(t * FOURIER_GROUP_DIM)
    return (jnp.cos(ang) * scale).astype(BF16), (jnp.sin(ang) * scale).astype(BF16)


def _fourier_call(u12, cmat, smat, tk=1024):
    bsz, t, w2 = u12.shape
    w = w2 // 2
    tk = min(tk, t)
    return pl.pallas_call(
        _fourier_kernel,
        out_shape=jax.ShapeDtypeStruct((bsz, t, w), BF16),
        grid=(t // tk, bsz),
        in_specs=[
            pl.BlockSpec((tk, t), lambda i, b: (i, 0)),
            pl.BlockSpec((tk, t), lambda i, b: (i, 0)),
            pl.BlockSpec((1, t, w), lambda i, b: (b, 0, 0)),
            pl.BlockSpec((1, t, w), lambda i, b: (b, 0, 1)),
        ],
        out_specs=pl.BlockSpec((1, tk, w), lambda i, b: (b, i, 0)),
        compiler_params=_cparams(("arbitrary", "arbitrary")),
        name="fourier_dft",
    )(cmat, smat, u12, u12)


def _outproj_kernel(f_ref, o_ref, x_ref, ga_ref, lg_ref, lb_ref, sh_ref, sc_ref, w_ref, wr_ref,
                    xo_ref, h_ref, lg_out_ref, *, alpha):
    fw = f_ref.shape[2]
    o = jnp.concatenate([o_ref[0, j] for j in range(o_ref.shape[1])], axis=1)
    proj = _dot(f_ref[0], w_ref[:fw, :]) + _dot(o, w_ref[fw:, :])
    z = alpha * x_ref[0] + ga_ref[0] * proj
    x = _ln(z) * lg_ref[...] + lb_ref[...]
    xo_ref[0] = x
    h = _ln(x) * (1.0 + sc_ref[0]) + sh_ref[0]
    hh, hl = _split_bf16(h)
    h_ref[0] = hh
    r = _dot_nt(wr_ref[...], hh)
    r2 = _dot_nt(wr_ref[:N_EXPERTS, :], hl)
    lg_out_ref[...] = r[:N_EXPERTS] + r[N_EXPERTS:] + r2


def _outproj_call(f, o, x, g_a, ln_g, ln_b, shift, scale, w_out, wr_hilo, alpha, tm=256):
    bsz, t, d = x.shape
    tm = min(tm, t)
    nt = t // tm
    row = lambda w: pl.BlockSpec((1, tm, w), lambda b, i: (b, i, 0))
    vec = pl.BlockSpec((1, 1, d), lambda b, i: (b, 0, 0))
    cst = lambda shp: pl.BlockSpec(shp, lambda b, i: (0,) * len(shp))
    return pl.pallas_call(
        functools.partial(_outproj_kernel, alpha=alpha),
        out_shape=(
            jax.ShapeDtypeStruct((bsz, t, d), F32),
            jax.ShapeDtypeStruct((bsz, t, d), BF16),
            jax.ShapeDtypeStruct((N_EXPERTS, bsz * t), F32),
        ),
        grid=(bsz, nt),
        in_specs=[row(f.shape[2]), pl.BlockSpec((1, o.shape[1], tm, V_DIM), lambda b, i: (b, 0, i, 0)), row(d), vec, cst((1, d)), cst((1, d)), vec, vec,
                  cst(w_out.shape), cst(wr_hilo.shape)],
        out_specs=(row(d), row(d), pl.BlockSpec((N_EXPERTS, tm), lambda b, i: (0, b * nt + i))),
        compiler_params=_cparams(("arbitrary", "arbitrary")),
        name="outproj",
    )(f, o, x, g_a, ln_g.reshape(1, d), ln_b.reshape(1, d), shift, scale, w_out, wr_hilo)


def _moe_kernel(e0_ref, e1_ref, valid_ref, fresh_ref, x_ref, wt_ref, wg0, wu0, wd0, wg1, wu1, wd1, y_ref,
                wg_s, wu_s, wd_s):
    i = pl.program_id(0)

    @pl.when(fresh_ref[i] != 0)
    def _():
        for slot, (wg, wu, wd) in enumerate(((wg0, wu0, wd0), (wg1, wu1, wd1))):
            wg_s[slot] = wg[0].astype(BF16)
            wu_s[slot] = wu[0].astype(BF16)
            wd_s[slot] = wd[0].astype(BF16)

    @pl.when(valid_ref[i] != 0)
    def _():
        x = x_ref[...]

        def expert(slot):
            g = _dot(x, wg_s[slot])
            u = _dot(x, wu_s[slot])
            a = (g * jax.nn.sigmoid(g) * u).astype(BF16)
            return _dot(a, wd_s[slot])

        wt = wt_ref[...]
        y = wt[:, 0:1] * expert(0) + wt[:, 1:2] * expert(1)
        y_ref[...] = y.astype(BF16)

    @pl.when(valid_ref[i] == 0)
    def _():
        y_ref[...] = jnp.zeros_like(y_ref)


def _moe_call(xs, wts, tile_e0, tile_e1, tile_valid, tile_fresh, w_gate, w_up, w_down, tmm):
    tpad, d = xs.shape
    de = w_gate.shape[2]
    n_tiles = tpad // tmm
    wspec0 = lambda shp: pl.BlockSpec((1,) + shp, lambda i, e0, e1, vl, fr: (e0[i], 0, 0))
    wspec1 = lambda shp: pl.BlockSpec((1,) + shp, lambda i, e0, e1, vl, fr: (e1[i], 0, 0))
    grid_spec = pltpu.PrefetchScalarGridSpec(
        num_scalar_prefetch=4,
        grid=(n_tiles,),
        in_specs=[
            pl.BlockSpec((tmm, d), lambda i, e0, e1, vl, fr: (i, 0)),
            pl.BlockSpec((tmm, wts.shape[1]), lambda i, e0, e1, vl, fr: (i, 0)),
            wspec0((d, de)), wspec0((d, de)), wspec0((de, d)),
            wspec1((d, de)), wspec1((d, de)), wspec1((de, d)),
        ],
        out_specs=pl.BlockSpec((tmm, d), lambda i, e0, e1, vl, fr: (i, 0)),
        scratch_shapes=[pltpu.VMEM((2, d, de), BF16), pltpu.VMEM((2, d, de), BF16), pltpu.VMEM((2, de, d), BF16)],
    )
    return pl.pallas_call(
        _moe_kernel,
        out_shape=jax.ShapeDtypeStruct((tpad, d), BF16),
        grid_spec=grid_spec,
        compiler_params=_cparams(("arbitrary",)),
        name="moe_ffn",
    )(tile_e0, tile_e1, tile_valid, tile_fresh, xs, wts, w_gate, w_up, w_down, w_gate, w_up, w_down)


def _route(logits_t, router_bias):
    logits = logits_t.T
    scores = jax.nn.sigmoid(logits)
    sel = scores + router_bias.astype(F32)
    grp_sel = sel.reshape(sel.shape[0], N_GROUPS, EXPERTS_PER_GROUP)
    grp_score = lax.top_k(grp_sel, TOP_K)[0].sum(-1)
    best_grp = jnp.argmax(grp_score, axis=-1)
    in_grp = (jnp.arange(N_EXPERTS) // EXPERTS_PER_GROUP) == best_grp[:, None]
    masked = jnp.where(in_grp, sel, -jnp.inf)
    _, idx = lax.top_k(masked, TOP_K)
    w = jnp.take_along_axis(scores, idx, axis=-1)
    w = w / jnp.sum(w, axis=-1, keepdims=True)
    swap = idx[:, 0] > idx[:, 1]
    e0 = jnp.where(swap, idx[:, 1], idx[:, 0]).astype(jnp.int32)
    e1 = jnp.where(swap, idx[:, 0], idx[:, 1]).astype(jnp.int32)
    w0 = jnp.where(swap, w[:, 1], w[:, 0])
    w1 = jnp.where(swap, w[:, 0], w[:, 1])
    return e0, e1, w0, w1


def _pair_tables():
    e0s, e1s = [], []
    for g in range(N_GROUPS):
        for a in range(EXPERTS_PER_GROUP):
            for b in range(a + 1, EXPERTS_PER_GROUP):
                e0s.append(g * EXPERTS_PER_GROUP + a)
                e1s.append(g * EXPERTS_PER_GROUP + b)
    return np.asarray(e0s, np.int32), np.asarray(e1s, np.int32)


def _moe_ffn(h2, logits_t, router_bias, w_gate, w_up, w_down, tmm=256):
    t, d = h2.shape
    e0, e1, w0, w1 = _route(logits_t, router_bias)
    a = e0 % EXPERTS_PER_GROUP
    b = e1 % EXPERTS_PER_GROUP
    pairs_per_group = N_PAIR_CLASSES // N_GROUPS
    cls = (e0 // EXPERTS_PER_GROUP) * pairs_per_group + a * (2 * EXPERTS_PER_GROUP - 1 - a) // 2 + (b - a - 1)
    chunk = 256
    onehot = (cls[:, None] == jnp.arange(N_PAIR_CLASSES, dtype=jnp.int32)[None, :])
    oh = onehot.astype(BF16).reshape(t // chunk, chunk, N_PAIR_CLASSES)
    tri = jnp.asarray(np.tril(np.ones((chunk, chunk), np.float32), -1), BF16)
    within = jnp.einsum("ij,njc->nic", tri, oh, preferred_element_type=F32)
    chunk_counts = jnp.sum(oh.astype(F32), axis=1)
    chunk_base = jnp.cumsum(chunk_counts, axis=0) - chunk_counts
    prefix = (within + chunk_base[:, None, :]).reshape(t, N_PAIR_CLASSES)
    rank = jnp.sum(jnp.where(onehot, prefix, 0.0), axis=1).astype(jnp.int32)
    counts = jnp.sum(chunk_counts, axis=0).astype(jnp.int32)
    padded = ((counts + tmm - 1) // tmm) * tmm
    ends = jnp.cumsum(padded)
    starts = ends - padded
    pos = starts[cls] + rank
    n_tiles = t // tmm + N_PAIR_CLASSES
    tpad = n_tiles * tmm
    total = ends[-1]
    tile_start = jnp.arange(n_tiles, dtype=jnp.int32) * tmm
    tile_valid = (tile_start < total).astype(jnp.int32)
    tile_cls = jnp.searchsorted(ends, jnp.minimum(tile_start, total - 1), side="right").astype(jnp.int32)
    tile_cls = jnp.minimum(tile_cls, N_PAIR_CLASSES - 1)
    pe0, pe1 = _pair_tables()
    tile_e0 = jnp.asarray(pe0)[tile_cls]
    tile_e1 = jnp.asarray(pe1)[tile_cls]
    tile_fresh = jnp.concatenate([jnp.ones((1,), jnp.int32),
                                  (tile_cls[1:] != tile_cls[:-1]).astype(jnp.int32)])
    payload = jnp.stack([w0, w1, jnp.arange(t, dtype=F32)] + [jnp.zeros((t,), F32)] * 5, axis=1)
    wts = jnp.zeros((tpad, 8), F32).at[pos].set(payload)
    tok_of_pos = wts[:, 2].astype(jnp.int32)
    xs = jnp.take(h2, tok_of_pos, axis=0)
    ys = _moe_call(xs, wts, tile_e0, tile_e1, tile_valid, tile_fresh, w_gate, w_up, w_down, tmm)
    return jnp.take(ys, pos, axis=0)


def _ffn_ln_kernel(x_ref, y_ref, gf_ref, lg_ref, lb_ref, o_ref, *, alpha):
    z = alpha * x_ref[0] + gf_ref[0] * y_ref[0].astype(F32)
    o_ref[0] = _ln(z) * lg_ref[...] + lb_ref[...]


def _ffn_ln_call(x, y, g_f, ln_g, ln_b, alpha, tm=512):
    bsz, t, d = x.shape
    tm = min(tm, t)
    row = pl.BlockSpec((1, tm, d), lambda b, i: (b, i, 0))
    vec = pl.BlockSpec((1, 1, d), lambda b, i: (b, 0, 0))
    cst = pl.BlockSpec((1, d), lambda b, i: (0, 0))
    return pl.pallas_call(
        functools.partial(_ffn_ln_kernel, alpha=alpha),
        out_shape=jax.ShapeDtypeStruct((bsz, t, d), F32),
        grid=(bsz, t // tm),
        in_specs=[row, row, vec, cst, cst],
        out_specs=row,
        compiler_params=_cparams(("arbitrary", "arbitrary")),
        name="ffn_ln",
    )(x, y, g_f, ln_g.reshape(1, d), ln_b.reshape(1, d))


def _rope_tables(s):
    rows = s // GRID_W
    row = jnp.repeat(jnp.arange(rows), GRID_W).astype(F32)
    col = jnp.tile(jnp.arange(GRID_W), rows).astype(F32)
    freqs = ROPE_BASE ** (-jnp.arange(0, ROPE_AXIS_DIM, 2, dtype=F32) / ROPE_AXIS_DIM)
    ang_row = row[:, None] * freqs
    ang_col = col[:, None] * freqs

    def head(r, c, sign):
        return jnp.concatenate([sign * r, r, sign * c, c], axis=1)

    cos = head(jnp.cos(ang_row), jnp.cos(ang_col), 1.0)
    sin = head(jnp.sin(ang_row), jnp.sin(ang_col), -1.0)
    cos = jnp.concatenate([cos, cos], axis=1)
    sin = jnp.concatenate([sin, sin], axis=1)
    return cos, sin


def kernel(x, c, ctx, c_ctx, w_mod, b_mod, w_in, w_fourier, lam_qk, subln_g, w_out, ln_attn_g, ln_attn_b,
           ln_ffn_g, ln_ffn_b, w_router, router_bias, w_gate, w_up, w_down):
    bsz, s, d = x.shape
    l_ctx = ctx.shape[1]
    depth = w_mod.shape[0]
    alpha = (2 * depth) ** 0.25
    qscale = LOG2E * HEAD_DIM ** -0.5

    pad = (-(bsz + 1)) % 8
    cc = jnp.concatenate([c, c_ctx[None, :], jnp.zeros((pad, d), F32)], axis=0)
    mod = _mod_call(cc, w_mod, b_mod)

    ab = _fprep_call(w_fourier)
    v_off = w_in.shape[2] - (w_in.shape[2] - FOURIER_WIDTH) // 3
    w_uqk = w_in[:, :, :v_off].astype(BF16)
    w_vt = jnp.swapaxes(w_in[:, :, v_off:], 1, 2).astype(BF16)
    w_out_b = w_out.astype(BF16)
    wr_t = w_router.T.astype(F32)
    wr_hi = wr_t.astype(BF16)
    wr_lo = (wr_t - wr_hi.astype(F32)).astype(BF16)
    wr_hilo = jnp.concatenate([wr_hi, wr_lo], axis=0)

    cos, sin = _rope_tables(s)
    tabs_lat = (cos * qscale, sin * qscale, cos, sin)
    ones = jnp.ones((l_ctx, LANES), F32)
    zeros = jnp.zeros((l_ctx, LANES), F32)
    tabs_ctx = (ones * qscale, zeros, ones, zeros)
    dft_lat = _dft_mats(s)
    dft_ctx = _dft_mats(l_ctx)

    xc = ctx
    prev = None
    prev_c = None
    for l in range(depth):
        last = l == depth - 1
        lam_init = 0.8 - 0.6 * math.exp(-0.3 * l)
        lq = lam_qk[l].astype(F32)
        lam = jnp.exp(jnp.sum(lq[0] * lq[1])) - jnp.exp(jnp.sum(lq[2] * lq[3])) + lam_init

        m_lat = mod[l, :bsz].reshape(bsz, 1, 6, d)
        sh_a, sc_a, g_a, sh_f, sc_f, g_f = [m_lat[:, :, i, :] for i in range(6)]
        m_ctx = jnp.broadcast_to(mod[l, bsz].reshape(1, 1, 6, d), (bsz, 1, 6, d))
        csh_a, csc_a, cg_a, csh_f, csc_f, cg_f = [m_ctx[:, :, i, :] for i in range(6)]

        outs = _inproj_call(x, sh_a, sc_a, w_uqk[l], w_vt[l], ab[l], tabs_lat, prev=prev, alpha=alpha)
        u12, q, k, v = outs[:4]
        if prev is not None:
            x = outs[4]
        outs_c = _inproj_call(xc, csh_a, csc_a, w_uqk[l], w_vt[l], ab[l], tabs_ctx, prev=prev_c, alpha=alpha)
        u12c, qc, kc, vc = outs_c[:4]
        if prev_c is not None:
            xc = outs_c[4]

        o_attn = _attn_call(q, [(kc, vc), (k, v)], lam, subln_g[l], lam_init)
        f = _fourier_call(u12, *dft_lat)
        x, h2, logits_t = _outproj_call(f, o_attn, x, g_a, ln_attn_g[l], ln_attn_b[l], sh_f, sc_f,
                                        w_out_b[l], wr_hilo, alpha)
        h2 = h2.reshape(bsz * s, d)

        if not last:
            oc_attn = _attn_call(qc, [(kc, vc)], lam, subln_g[l], lam_init)
            fc = _fourier_call(u12c, *dft_ctx)
            xc, h2c, logits_c = _outproj_call(fc, oc_attn, xc, cg_a, ln_attn_g[l], ln_attn_b[l], csh_f, csc_f,
                                              w_out_b[l], wr_hilo, alpha)
            h2_all = jnp.concatenate([h2c.reshape(bsz * l_ctx, d), h2], axis=0)
            logits_all = jnp.concatenate([logits_c, logits_t], axis=1)
            y_all = _moe_ffn(h2_all, logits_all, router_bias, w_gate[l], w_up[l], w_down[l])
            yc = y_all[:bsz * l_ctx].reshape(bsz, l_ctx, d)
            y = y_all[bsz * l_ctx:].reshape(bsz, s, d)
            prev_c = (yc, cg_f, ln_ffn_g[l], ln_ffn_b[l])
        else:
            y = _moe_ffn(h2, logits_t, router_bias, w_gate[l], w_up[l], w_down[l]).reshape(bsz, s, d)
        prev = (y, g_f, ln_ffn_g[l], ln_ffn_b[l])

    y, g_f, lg, lb = prev
    return _ffn_ln_call(x, y, g_f, lg, lb, alpha)
```

```python
import functools
import math

import jax
import jax.numpy as jnp
import numpy as np
from jax import lax
from jax.experimental import pallas as pl
from jax.experimental.pallas import tpu as pltpu

F32 = jnp.float32
BF16 = jnp.bfloat16

GRID_W = 64
FOURIER_WIDTH = 256
FOURIER_GROUPS = 4
FOURIER_GROUP_DIM = FOURIER_WIDTH // FOURIER_GROUPS
HEAD_DIM = 64
V_DIM = 2 * HEAD_DIM
ROPE_BASE = 10000.0
ROPE_AXIS_DIM = HEAD_DIM // 2
N_EXPERTS = 16
N_GROUPS = 4
EXPERTS_PER_GROUP = N_EXPERTS // N_GROUPS
N_PAIR_CLASSES = N_GROUPS * (EXPERTS_PER_GROUP * (EXPERTS_PER_GROUP - 1) // 2)
TOP_K = 2
LN_EPS = 1e-5
LOG2E = math.log2(math.e)

LANES = 128
SUBLANES_BF16 = 16
VMEM_LIMIT = 48 * 1024 * 1024


def _cparams(sem):
    return pltpu.CompilerParams(dimension_semantics=sem, vmem_limit_bytes=VMEM_LIMIT)


def _split_bf16(a):
    hi = a.astype(BF16)
    lo = (a - hi.astype(F32)).astype(BF16)
    return hi, lo


def _dot(a, b):
    return jnp.dot(a, b, preferred_element_type=F32)


def _dot_nt(a, b):
    return lax.dot_general(a, b, (((1,), (1,)), ((), ())), preferred_element_type=F32)


def _dot3(a, w):
    ah, al = _split_bf16(a)
    wh, wl = _split_bf16(w)
    return _dot(ah, wh) + _dot(ah, wl) + _dot(al, wh)


def _ln(x):
    mu = jnp.mean(x, axis=-1, keepdims=True)
    xc = x - mu
    var = jnp.mean(xc * xc, axis=-1, keepdims=True)
    return xc * lax.rsqrt(var + LN_EPS)


def _mod_kernel(c_ref, w_ref, b_ref, o_ref):
    c = c_ref[...]
    s = c * jax.nn.sigmoid(c)
    o_ref[0] = _dot3(s, w_ref[0]) + b_ref[0]


def _mod_call(cc, w_mod, b_mod):
    depth, d, n = w_mod.shape
    r = cc.shape[0]
    tn = 1536
    return pl.pallas_call(
        _mod_kernel,
        out_shape=jax.ShapeDtypeStruct((depth, r, n), F32),
        grid=(depth, n // tn),
        in_specs=[
            pl.BlockSpec((r, d), lambda l, j: (0, 0)),
            pl.BlockSpec((1, d, tn), lambda l, j: (l, 0, j)),
            pl.BlockSpec((1, 1, tn), lambda l, j: (l, 0, j)),
        ],
        out_specs=pl.BlockSpec((1, r, tn), lambda l, j: (l, 0, j)),
        compiler_params=_cparams(("arbitrary", "arbitrary")),
        name="mod",
    )(cc, w_mod, b_mod.reshape(depth, 1, n))


def _fprep_kernel(c_ref, s_ref, w_ref, a_ref, b_ref):
    w = w_ref[0]
    a_ref[0] = _dot3(c_ref[...], w)
    b_ref[0] = _dot3(s_ref[...], w)


def _fprep_call(w_fourier):
    depth, g, c, _ = w_fourier.shape
    idx = np.arange(c)
    ang = 2.0 * np.pi * ((idx[:, None] * idx[None, :]) % c) / c
    c64 = jnp.asarray(np.cos(ang), F32)
    s64 = jnp.asarray(np.sin(ang), F32)
    wf = w_fourier.reshape(depth * g, c, c)
    spec = pl.BlockSpec((1, c, c), lambda i: (i, 0, 0))
    cst = pl.BlockSpec((c, c), lambda i: (0, 0))
    a, b = pl.pallas_call(
        _fprep_kernel,
        out_shape=(jax.ShapeDtypeStruct(wf.shape, F32),) * 2,
        grid=(depth * g,),
        in_specs=[cst, cst, spec],
        out_specs=(spec, spec),
        compiler_params=_cparams(("arbitrary",)),
        name="fourier_prep",
    )(c64, s64, wf)
    eye = jnp.eye(g, dtype=F32)

    def bd(m):
        m = m.reshape(depth, g, c, c)
        return (m[:, :, :, None, :] * eye[None, :, None, :, None]).reshape(depth, g * c, g * c)

    return jnp.concatenate([bd(a), bd(b)], axis=-1).astype(BF16)


def _inproj_kernel(*refs, fuse_prev, alpha):
    if fuse_prev:
        (x_ref, y_ref, gf_ref, lg_ref, lb_ref, sh_ref, sc_ref, w_ref, wvt_ref, ab_ref,
         cq_ref, sq_ref, ck_ref, sk_ref, u_ref, q_ref, k_ref, v_ref, xo_ref) = refs
        z = alpha * x_ref[0] + gf_ref[0] * _unpack_halves(y_ref[0])
        x = _ln(z) * lg_ref[...] + lb_ref[...]
        xo_ref[0] = x
    else:
        (x_ref, sh_ref, sc_ref, w_ref, wvt_ref, ab_ref,
         cq_ref, sq_ref, ck_ref, sk_ref, u_ref, q_ref, k_ref, v_ref) = refs
        x = x_ref[0]
    h = (_ln(x) * (1.0 + sc_ref[0]) + sh_ref[0]).astype(BF16)

    fw = FOURIER_WIDTH
    aw = (w_ref.shape[1] - fw) // 2
    nh = aw // LANES
    u = _dot(h, w_ref[:, :fw]).astype(BF16)
    u_ref[0] = _dot(u, ab_ref[...]).astype(BF16)

    tm = h.shape[0]
    lane = lax.broadcasted_iota(jnp.int32, (tm, LANES), 1)
    half = ROPE_AXIS_DIM // 2
    first_half = (lane & half) == 0

    def rope_store(p, cos, sin, dst_ref):
        for j in range(nh):
            pj = p[:, j * LANES:(j + 1) * LANES]
            swapped = jnp.where(first_half, pltpu.roll(pj, LANES - half, 1), pltpu.roll(pj, half, 1))
            dst_ref[0, j] = (pj * cos + swapped * sin).astype(BF16)

    rope_store(_dot(h, w_ref[:, fw:fw + aw]), cq_ref[...], sq_ref[...], q_ref)
    rope_store(_dot(h, w_ref[:, fw + aw:]), ck_ref[...], sk_ref[...], k_ref)
    vt = _dot_nt(wvt_ref[...], h)
    for j in range(nh):
        v_ref[0, j] = vt[j * V_DIM:(j + 1) * V_DIM, :].astype(BF16)


def _inproj_call(x, shift, scale, w_uqk, w_vt, ab, tabs, prev=None, alpha=1.0, tm=256):
    bsz, t, d = x.shape
    aw = w_vt.shape[0]
    nh = aw // V_DIM
    tm = min(tm, t)
    grid = (bsz, t // tm)
    row = pl.BlockSpec((1, tm, d), lambda b, i: (b, i, 0))
    vec = pl.BlockSpec((1, 1, d), lambda b, i: (b, 0, 0))
    cst = lambda shp: pl.BlockSpec(shp, lambda b, i: (0,) * len(shp))
    tab = pl.BlockSpec((tm, LANES), lambda b, i: (i, 0))
    in_specs, args = [row], [x]
    if prev is not None:
        y, gf, lg, lb = prev
        in_specs += [pl.BlockSpec((1, tm, d // 2), lambda b, i: (b, i, 0)), vec, cst((1, d)), cst((1, d))]
        args += [y, gf, lg.reshape(1, d), lb.reshape(1, d)]
    in_specs += [vec, vec, cst(w_uqk.shape), cst(w_vt.shape), cst(ab.shape), tab, tab, tab, tab]
    args += [shift, scale, w_uqk, w_vt, ab, *tabs]
    head_rows = pl.BlockSpec((1, nh, tm, V_DIM), lambda b, i: (b, 0, i, 0))
    out_shape = [
        jax.ShapeDtypeStruct((bsz, t, 2 * FOURIER_WIDTH), BF16),
        jax.ShapeDtypeStruct((bsz, nh, t, V_DIM), BF16),
        jax.ShapeDtypeStruct((bsz, nh, t, V_DIM), BF16),
        jax.ShapeDtypeStruct((bsz, nh, V_DIM, t), BF16),
    ]
    out_specs = [
        pl.BlockSpec((1, tm, 2 * FOURIER_WIDTH), lambda b, i: (b, i, 0)),
        head_rows,
        head_rows,
        pl.BlockSpec((1, nh, V_DIM, tm), lambda b, i: (b, 0, 0, i)),
    ]
    if prev is not None:
        out_shape.append(jax.ShapeDtypeStruct((bsz, t, d), F32))
        out_specs.append(row)
    return pl.pallas_call(
        functools.partial(_inproj_kernel, fuse_prev=prev is not None, alpha=alpha),
        out_shape=tuple(out_shape),
        grid=grid,
        in_specs=in_specs,
        out_specs=tuple(out_specs),
        compiler_params=_cparams(("arbitrary", "arbitrary")),
        name="inproj",
    )(*args)


def _attn_kernel(lam_ref, q_ref, *refs, n_seg, norm_scale):
    kv_refs = refs[:2 * n_seg]
    g_ref, o_ref, s_buf, m_buf = refs[2 * n_seg:]
    nh, tq = q_ref.shape[1], q_ref.shape[2]
    lane = lax.broadcasted_iota(jnp.int32, (tq, LANES), 1)
    lam = lam_ref[0]
    gain = g_ref[...] * norm_scale
    seg_len = [kv_refs[2 * j].shape[2] for j in range(n_seg)]
    seg_off = [sum(seg_len[:j]) for j in range(n_seg)]

    def scores(h, sub):
        qb = q_ref[0, h]
        qs = jnp.where((lane < HEAD_DIM) == (sub == 0), qb, jnp.zeros_like(qb))
        m = None
        for j in range(n_seg):
            s = _dot_nt(kv_refs[2 * j][0, h], qs)
            s_buf[sub, seg_off[j]:seg_off[j] + seg_len[j], :] = s
            mj = jnp.max(s, axis=0, keepdims=True)
            m = mj if m is None else jnp.maximum(m, mj)
        m_buf[sub] = jnp.broadcast_to(m, m_buf.shape[1:])

    def attend(h, sub):
        m = m_buf[sub][0:1]
        acc = None
        for j in range(n_seg):
            p = jnp.exp2(s_buf[sub, seg_off[j]:seg_off[j] + seg_len[j], :] - m).astype(BF16)
            vt = kv_refs[2 * j + 1][0, h]
            vext = jnp.concatenate([vt, jnp.ones((SUBLANES_BF16, vt.shape[1]), BF16)], axis=0)
            aj = _dot(vext, p)
            acc = aj if acc is None else acc + aj
        return acc[:V_DIM] / acc[V_DIM:V_DIM + 1]

    def finish(h, o0, o1):
        o = o0 - lam * o1
        o = o * lax.rsqrt(jnp.mean(o * o, axis=0, keepdims=True) + LN_EPS)
        o_ref[0, h] = (o.T * gain).astype(BF16)

    scores(0, 0)

    def head(h, carry):
        scores(h, 1)
        o0 = attend(h, 0)
        scores(h + 1, 0)
        o1 = attend(h, 1)
        finish(h, o0, o1)
        return carry

    lax.fori_loop(0, nh - 1, head, 0)
    scores(nh - 1, 1)
    o0 = attend(nh - 1, 0)
    o1 = attend(nh - 1, 1)
    finish(nh - 1, o0, o1)


def _attn_call(q, kv_segs, lam, g, lam_init, tq=512):
    bsz, nh, t, _ = q.shape
    tq = min(tq, t)
    grid = (bsz, t // tq)
    qspec = pl.BlockSpec((1, nh, tq, V_DIM), lambda b, i: (b, 0, i, 0))
    in_specs = [pl.BlockSpec(memory_space=pltpu.SMEM), qspec]
    args = [lam.reshape(1).astype(F32), q]
    for k, vt in kv_segs:
        in_specs += [pl.BlockSpec((1,) + k.shape[1:], lambda b, i: (b, 0, 0, 0)),
                     pl.BlockSpec((1,) + vt.shape[1:], lambda b, i: (b, 0, 0, 0))]
        args += [k, vt]
    in_specs.append(pl.BlockSpec((1, V_DIM), lambda b, i: (0, 0)))
    args.append(g.reshape(1, V_DIM).astype(F32))
    n_keys = sum(k.shape[2] for k, _ in kv_segs)
    return pl.pallas_call(
        functools.partial(_attn_kernel, n_seg=len(kv_segs), norm_scale=1.0 - lam_init),
        out_shape=jax.ShapeDtypeStruct(q.shape, BF16),
        grid=grid,
        in_specs=in_specs,
        out_specs=qspec,
        scratch_shapes=[pltpu.VMEM((2, n_keys, tq), F32), pltpu.VMEM((2, 8, tq), F32)],
        compiler_params=_cparams(("arbitrary", "arbitrary")),
        name="diff_attn",
    )(*args)


def _fourier_kernel(c_ref, s_ref, u1_ref, u2_ref, o_ref):
    o_ref[0] = (_dot(c_ref[...], u1_ref[0]) - _dot(s_ref[...], u2_ref[0])).astype(BF16)


def _dft_mats(t):
    idx = jnp.arange(t, dtype=jnp.int32)
    ang = ((idx[:, None] * idx[None, :]) % t).astype(F32) * (2.0 * math.pi / t)
    scale = 1.0 / math.sqrt(t * FOURIER_GROUP_DIM)
    return (jnp.cos(ang) * scale).astype(BF16), (jnp.sin(ang) * scale).astype(BF16)


def _fourier_call(u12, cmat, smat, tk=1024):
    bsz, t, w2 = u12.shape
    w = w2 // 2
    tk = min(tk, t)
    return pl.pallas_call(
        _fourier_kernel,
        out_shape=jax.ShapeDtypeStruct((bsz, t, w), BF16),
        grid=(t // tk, bsz),
        in_specs=[
            pl.BlockSpec((tk, t), lambda i, b: (i, 0)),
            pl.BlockSpec((tk, t), lambda i, b: (i, 0)),
            pl.BlockSpec((1, t, w), lambda i, b: (b, 0, 0)),
            pl.BlockSpec((1, t, w), lambda i, b: (b, 0, 1)),
        ],
        out_specs=pl.BlockSpec((1, tk, w), lambda i, b: (b, i, 0)),
        compiler_params=_cparams(("arbitrary", "arbitrary")),
        name="fourier_dft",
    )(cmat, smat, u12, u12)


def _pack_halves(v):
    n = v.shape[1] // 2
    bits = lax.bitcast_convert_type(v, jnp.uint32)
    return bits[:, :n] | (bits[:, n:] >> 16)


def _unpack_halves(u):
    hi = lax.bitcast_convert_type(u & jnp.uint32(0xFFFF0000), F32)
    lo = lax.bitcast_convert_type(u << 16, F32)
    return jnp.concatenate([hi, lo], axis=1)


def _route_rows(logits, bias):
    score = jax.nn.sigmoid(logits)
    sel = score + bias
    r = [sel[j:j + 1, :] for j in range(N_EXPERTS)]
    s = [score[j:j + 1, :] for j in range(N_EXPERTS)]
    npg = EXPERTS_PER_GROUP
    best = None
    for g in range(N_GROUPS):
        v = r[g * npg:(g + 1) * npg]
        pair_sums = [v[a] + v[b] for a in range(npg) for b in range(a + 1, npg)]
        tg = functools.reduce(jnp.maximum, pair_sums)
        if best is None:
            best, bg = tg, jnp.zeros_like(tg)
        else:
            upd = tg > best
            best = jnp.where(upd, tg, best)
            bg = jnp.where(upd, float(g), bg)

    def of_group(rows, j):
        out = rows[j]
        for g in range(1, N_GROUPS):
            out = jnp.where(bg == float(g), rows[g * npg + j], out)
        return out

    v = [of_group(r, j) for j in range(npg)]
    sv = [of_group(s, j) for j in range(npg)]

    def first_argmax(vals):
        m = functools.reduce(jnp.maximum, vals)
        idx = jnp.full_like(m, float(npg - 1))
        for j in range(npg - 2, -1, -1):
            idx = jnp.where(vals[j] == m, float(j), idx)
        return idx

    i1 = first_argmax(v)
    i2 = first_argmax([jnp.where(i1 == float(j), -jnp.inf, v[j]) for j in range(npg)])
    lo = jnp.minimum(i1, i2)
    hi = jnp.maximum(i1, i2)

    def pick(vals, idx):
        out = vals[0]
        for j in range(1, npg):
            out = jnp.where(idx == float(j), vals[j], out)
        return out

    s_lo, s_hi = pick(sv, lo), pick(sv, hi)
    den = s_lo + s_hi
    pair = lo * (2.0 * npg - 1.0 - lo) * 0.5 + (hi - lo - 1.0)
    cls = bg * float(N_PAIR_CLASSES // N_GROUPS) + pair
    return cls, s_lo / den, s_hi / den


def _outproj_kernel(f_ref, o_ref, x_ref, ga_ref, lg_ref, lb_ref, sh_ref, sc_ref, w_ref, wr_ref, rb_ref, tri_ref,
                    cin_ref, xo_ref, hx_ref, rt_ref, cnt_ref, *, alpha):
    first = (pl.program_id(0) == 0) & (pl.program_id(1) == 0)

    @pl.when(first)
    def _():
        cnt_ref[...] = cin_ref[...]

    fw = f_ref.shape[2]
    o = jnp.concatenate([o_ref[0, j] for j in range(o_ref.shape[1])], axis=1)
    proj = _dot(f_ref[0], w_ref[:fw, :]) + _dot(o, w_ref[fw:, :])
    z = alpha * x_ref[0] + ga_ref[0] * proj
    x = _ln(z) * lg_ref[...] + lb_ref[...]
    xo_ref[0] = x
    h = _ln(x) * (1.0 + sc_ref[0]) + sh_ref[0]
    hh, hl = _split_bf16(h)
    r = _dot_nt(wr_ref[...], hh)
    r2 = _dot_nt(wr_ref[:N_EXPERTS, :], hl)
    logits = r[:N_EXPERTS] + r[N_EXPERTS:] + r2
    cls, w0, w1 = _route_rows(logits, rb_ref[...])

    tm = h.shape[0]
    ncls = cnt_ref.shape[0]
    cls_iota = lax.broadcasted_iota(jnp.int32, (ncls, tm), 0).astype(F32)
    onehot = cls_iota == cls
    oh = onehot.astype(F32)
    prefix = _dot(oh.astype(BF16), tri_ref[...])
    counts = cnt_ref[...]
    base = jnp.concatenate([counts] * (tm // LANES), axis=1)
    rank = jnp.sum(jnp.where(onehot, prefix + base, 0.0), axis=0, keepdims=True)
    cnt_ref[...] = counts + jnp.sum(oh, axis=1, keepdims=True)

    row_iota = lax.broadcasted_iota(jnp.int32, (rt_ref.shape[0], tm), 0)
    rt_ref[...] = jnp.where(row_iota == 0, cls, jnp.where(row_iota == 1, rank, 0.0))

    wrow_iota = lax.broadcasted_iota(jnp.int32, (LANES, tm), 0)
    wcols = jnp.where(wrow_iota == 0, w0, jnp.where(wrow_iota == 1, w1, 0.0)).T
    hx_ref[0] = jnp.concatenate([_pack_halves(hh.astype(F32)), lax.bitcast_convert_type(wcols, jnp.uint32)],
                                axis=1)


def _outproj_call(f, o, x, g_a, ln_g, ln_b, shift, scale, w_out, wr_hilo, router_bias, counts_in, alpha, tm=256):
    bsz, t, d = x.shape
    tm = min(tm, t)
    nt = t // tm
    row = lambda w: pl.BlockSpec((1, tm, w), lambda b, i: (b, i, 0))
    vec = pl.BlockSpec((1, 1, d), lambda b, i: (b, 0, 0))
    cst = lambda shp: pl.BlockSpec(shp, lambda b, i: (0,) * len(shp))
    tri = jnp.asarray(np.triu(np.ones((tm, tm), np.float32), 1), BF16)
    wrow = d // 2 + LANES
    return pl.pallas_call(
        functools.partial(_outproj_kernel, alpha=alpha),
        out_shape=(
            jax.ShapeDtypeStruct((bsz, t, d), F32),
            jax.ShapeDtypeStruct((bsz, t, wrow), jnp.uint32),
            jax.ShapeDtypeStruct((8, bsz * t), F32),
            jax.ShapeDtypeStruct(counts_in.shape, F32),
        ),
        grid=(bsz, nt),
        in_specs=[row(f.shape[2]), pl.BlockSpec((1, o.shape[1], tm, V_DIM), lambda b, i: (b, 0, i, 0)), row(d),
                  vec, cst((1, d)), cst((1, d)), vec, vec, cst(w_out.shape), cst(wr_hilo.shape),
                  cst((N_EXPERTS, 1)), cst((tm, tm)), cst(counts_in.shape)],
        out_specs=(row(d), row(wrow), pl.BlockSpec((8, tm), lambda b, i: (0, b * nt + i)),
                   cst(counts_in.shape)),
        compiler_params=_cparams(("arbitrary", "arbitrary")),
        name="outproj",
    )(f, o, x, g_a, ln_g.reshape(1, d), ln_b.reshape(1, d), shift, scale, w_out, wr_hilo,
      router_bias.reshape(N_EXPERTS, 1).astype(F32), tri, counts_in)


def _row_copies(n_rows, src_row, dst_row, sem):
    def start(r, carry):
        pltpu.make_async_copy(src_row(r), dst_row(r), sem).start()
        return carry

    def wait(r, carry):
        pltpu.make_async_copy(src_row(0), dst_row(0), sem).wait()
        return carry

    lax.fori_loop(0, n_rows, start, 0, unroll=8)
    lax.fori_loop(0, n_rows, wait, 0, unroll=8)


def _scatter_rows_kernel(pos_ref, src_ref, dst_in_ref, dst_ref, sem):
    del dst_in_ref
    n = src_ref.shape[0]
    _row_copies(n, lambda r: src_ref.at[pl.ds(r, 1), :],
                lambda r: dst_ref.at[pl.ds(pos_ref[0, 0, r], 1), :], sem)


def _scatter_rows_call(src, pos, dst, rows_per_step=512):
    t, w = src.shape
    rs = min(rows_per_step, t)
    return pl.pallas_call(
        _scatter_rows_kernel,
        out_shape=jax.ShapeDtypeStruct(dst.shape, dst.dtype),
        grid=(t // rs,),
        in_specs=[pl.BlockSpec((1, 1, rs), lambda i: (i, 0, 0), memory_space=pltpu.SMEM),
                  pl.BlockSpec((rs, w), lambda i: (i, 0)),
                  pl.BlockSpec(memory_space=pl.ANY)],
        out_specs=pl.BlockSpec(memory_space=pl.ANY),
        scratch_shapes=[pltpu.SemaphoreType.DMA(())],
        input_output_aliases={2: 0},
        compiler_params=_cparams(("arbitrary",)),
        name="scatter_rows",
    )(pos.reshape(t // rs, 1, rs), src, dst)


def _gather_rows_kernel(pos_ref, src_ref, dst_ref, sem):
    n = dst_ref.shape[0]
    _row_copies(n, lambda r: src_ref.at[pl.ds(pos_ref[0, 0, r], 1), :],
                lambda r: dst_ref.at[pl.ds(r, 1), :], sem)


def _gather_rows_call(src, pos, rows_per_step=512):
    t = pos.shape[0]
    w = src.shape[1]
    rs = min(rows_per_step, t)
    return pl.pallas_call(
        _gather_rows_kernel,
        out_shape=jax.ShapeDtypeStruct((t, w), src.dtype),
        grid=(t // rs,),
        in_specs=[pl.BlockSpec((1, 1, rs), lambda i: (i, 0, 0), memory_space=pltpu.SMEM),
                  pl.BlockSpec(memory_space=pl.ANY)],
        out_specs=pl.BlockSpec((rs, w), lambda i: (i, 0)),
        scratch_shapes=[pltpu.SemaphoreType.DMA(())],
        compiler_params=_cparams(("arbitrary",)),
        name="gather_rows",
    )(pos.reshape(t // rs, 1, rs), src)


def _moe_kernel(e0_ref, e1_ref, valid_ref, fresh_ref, x_ref, wg0, wu0, wd0, wg1, wu1, wd1, y_ref,
                wg_s, wu_s, wd_s):
    i = pl.program_id(0)
    half = wg_s.shape[1] // 2

    @pl.when(fresh_ref[i] != 0)
    def _():
        for slot, (wg, wu, wd) in enumerate(((wg0, wu0, wd0), (wg1, wu1, wd1))):
            wg_s[slot] = wg[0].astype(BF16)
            wu_s[slot] = wu[0].astype(BF16)
            wd_s[slot] = wd[0].astype(BF16)

    @pl.when(valid_ref[i] != 0)
    def _():
        xrow = x_ref[...]
        x = _unpack_halves(xrow[:, :half]).astype(BF16)
        wt = lax.bitcast_convert_type(xrow[:, half:], F32)

        def expert(slot):
            g = _dot(x, wg_s[slot])
            u = _dot(x, wu_s[slot])
            a = (g * jax.nn.sigmoid(g) * u).astype(BF16)
            return _dot(a, wd_s[slot])

        y = wt[:, 0:1] * expert(0) + wt[:, 1:2] * expert(1)
        y_ref[...] = _pack_halves(y.astype(BF16).astype(F32))

    @pl.when(valid_ref[i] == 0)
    def _():
        y_ref[...] = jnp.zeros_like(y_ref)


def _moe_call(xs, tile_e0, tile_e1, tile_valid, tile_fresh, w_gate, w_up, w_down, tmm):
    tpad, wrow = xs.shape
    _, d, de = w_gate.shape
    n_tiles = tpad // tmm
    wspec0 = lambda shp: pl.BlockSpec((1,) + shp, lambda i, e0, e1, vl, fr: (e0[i], 0, 0))
    wspec1 = lambda shp: pl.BlockSpec((1,) + shp, lambda i, e0, e1, vl, fr: (e1[i], 0, 0))
    grid_spec = pltpu.PrefetchScalarGridSpec(
        num_scalar_prefetch=4,
        grid=(n_tiles,),
        in_specs=[
            pl.BlockSpec((tmm, wrow), lambda i, e0, e1, vl, fr: (i, 0)),
            wspec0((d, de)), wspec0((d, de)), wspec0((de, d)),
            wspec1((d, de)), wspec1((d, de)), wspec1((de, d)),
        ],
        out_specs=pl.BlockSpec((tmm, d // 2), lambda i, e0, e1, vl, fr: (i, 0)),
        scratch_shapes=[pltpu.VMEM((2, d, de), BF16), pltpu.VMEM((2, d, de), BF16), pltpu.VMEM((2, de, d), BF16)],
    )
    return pl.pallas_call(
        _moe_kernel,
        out_shape=jax.ShapeDtypeStruct((tpad, d // 2), jnp.uint32),
        grid_spec=grid_spec,
        compiler_params=_cparams(("arbitrary",)),
        name="moe_ffn",
    )(tile_e0, tile_e1, tile_valid, tile_fresh, xs, w_gate, w_up, w_down, w_gate, w_up, w_down)


def _pair_tables():
    e0s, e1s = [], []
    for g in range(N_GROUPS):
        for a in range(EXPERTS_PER_GROUP):
            for b in range(a + 1, EXPERTS_PER_GROUP):
                e0s.append(g * EXPERTS_PER_GROUP + a)
                e1s.append(g * EXPERTS_PER_GROUP + b)
    return np.asarray(e0s, np.int32), np.asarray(e1s, np.int32)


def _moe_ffn(rows, routes, counts, w_gate, w_up, w_down, tmm=256):
    t_all = sum(r.shape[0] for r in rows)
    cnt = counts[:N_PAIR_CLASSES, 0].astype(jnp.int32)
    padded = ((cnt + tmm - 1) // tmm) * tmm
    ends = jnp.cumsum(padded)
    starts = ends - padded
    n_tiles = t_all // tmm + N_PAIR_CLASSES
    tpad = n_tiles * tmm
    total = ends[-1]
    tile_start = jnp.arange(n_tiles, dtype=jnp.int32) * tmm
    tile_valid = (tile_start < total).astype(jnp.int32)
    probe = jnp.minimum(tile_start, total - 1)
    tile_cls = jnp.sum((ends[None, :] <= probe[:, None]).astype(jnp.int32), axis=1)
    tile_cls = jnp.minimum(tile_cls, N_PAIR_CLASSES - 1)
    pe0, pe1 = _pair_tables()
    tile_e0 = jnp.asarray(pe0)[tile_cls]
    tile_e1 = jnp.asarray(pe1)[tile_cls]
    tile_fresh = jnp.concatenate([jnp.ones((1,), jnp.int32),
                                  (tile_cls[1:] != tile_cls[:-1]).astype(jnp.int32)])
    cls_ids = jnp.arange(N_PAIR_CLASSES, dtype=F32)[:, None]
    starts_f = starts.astype(F32)[:, None]
    poss = []
    for rt in routes:
        start_of_tok = jnp.sum(jnp.where(rt[0][None, :] == cls_ids, starts_f, 0.0), axis=0)
        poss.append((start_of_tok + rt[1]).astype(jnp.int32))
    xs = jnp.zeros((tpad, rows[0].shape[1]), jnp.uint32)
    for r, pos in zip(rows, poss):
        xs = _scatter_rows_call(r, pos, xs)
    ys = _moe_call(xs, tile_e0, tile_e1, tile_valid, tile_fresh, w_gate, w_up, w_down, tmm)
    return [_gather_rows_call(ys, pos) for pos in poss]


def _ffn_ln_kernel(x_ref, y_ref, gf_ref, lg_ref, lb_ref, o_ref, *, alpha):
    z = alpha * x_ref[0] + gf_ref[0] * _unpack_halves(y_ref[0])
    o_ref[0] = _ln(z) * lg_ref[...] + lb_ref[...]


def _ffn_ln_call(x, y, g_f, ln_g, ln_b, alpha, tm=512):
    bsz, t, d = x.shape
    tm = min(tm, t)
    row = pl.BlockSpec((1, tm, d), lambda b, i: (b, i, 0))
    vec = pl.BlockSpec((1, 1, d), lambda b, i: (b, 0, 0))
    cst = pl.BlockSpec((1, d), lambda b, i: (0, 0))
    return pl.pallas_call(
        functools.partial(_ffn_ln_kernel, alpha=alpha),
        out_shape=jax.ShapeDtypeStruct((bsz, t, d), F32),
        grid=(bsz, t // tm),
        in_specs=[row, pl.BlockSpec((1, tm, d // 2), lambda b, i: (b, i, 0)), vec, cst, cst],
        out_specs=row,
        compiler_params=_cparams(("arbitrary", "arbitrary")),
        name="ffn_ln",
    )(x, y, g_f, ln_g.reshape(1, d), ln_b.reshape(1, d))


def _rope_tables(s):
    rows = s // GRID_W
    row = jnp.repeat(jnp.arange(rows), GRID_W).astype(F32)
    col = jnp.tile(jnp.arange(GRID_W), rows).astype(F32)
    freqs = ROPE_BASE ** (-jnp.arange(0, ROPE_AXIS_DIM, 2, dtype=F32) / ROPE_AXIS_DIM)
    ang_row = row[:, None] * freqs
    ang_col = col[:, None] * freqs

    def head(r, c, sign):
        return jnp.concatenate([sign * r, r, sign * c, c], axis=1)

    cos = head(jnp.cos(ang_row), jnp.cos(ang_col), 1.0)
    sin = head(jnp.sin(ang_row), jnp.sin(ang_col), -1.0)
    cos = jnp.concatenate([cos, cos], axis=1)
    sin = jnp.concatenate([sin, sin], axis=1)
    return cos, sin


def kernel(x, c, ctx, c_ctx, w_mod, b_mod, w_in, w_fourier, lam_qk, subln_g, w_out, ln_attn_g, ln_attn_b,
           ln_ffn_g, ln_ffn_b, w_router, router_bias, w_gate, w_up, w_down):
    bsz, s, d = x.shape
    l_ctx = ctx.shape[1]
    depth = w_mod.shape[0]
    alpha = (2 * depth) ** 0.25
    qscale = LOG2E * HEAD_DIM ** -0.5

    pad = (-(bsz + 1)) % 8
    cc = jnp.concatenate([c, c_ctx[None, :], jnp.zeros((pad, d), F32)], axis=0)
    mod = _mod_call(cc, w_mod, b_mod)

    ab = _fprep_call(w_fourier)
    v_off = w_in.shape[2] - (w_in.shape[2] - FOURIER_WIDTH) // 3
    w_uqk = w_in[:, :, :v_off].astype(BF16)
    w_vt = jnp.swapaxes(w_in[:, :, v_off:], 1, 2).astype(BF16)
    w_out_b = w_out.astype(BF16)
    wr_t = w_router.T.astype(F32)
    wr_hi = wr_t.astype(BF16)
    wr_lo = (wr_t - wr_hi.astype(F32)).astype(BF16)
    wr_hilo = jnp.concatenate([wr_hi, wr_lo], axis=0)

    cos, sin = _rope_tables(s)
    tabs_lat = (cos * qscale, sin * qscale, cos, sin)
    ones = jnp.ones((l_ctx, LANES), F32)
    zeros = jnp.zeros((l_ctx, LANES), F32)
    tabs_ctx = (ones * qscale, zeros, ones, zeros)
    dft_lat = _dft_mats(s)
    dft_ctx = _dft_mats(l_ctx)

    xc = ctx
    prev = None
    prev_c = None
    for l in range(depth):
        last = l == depth - 1
        lam_init = 0.8 - 0.6 * math.exp(-0.3 * l)
        lq = lam_qk[l].astype(F32)
        lam = jnp.exp(jnp.sum(lq[0] * lq[1])) - jnp.exp(jnp.sum(lq[2] * lq[3])) + lam_init

        m_lat = mod[l, :bsz].reshape(bsz, 1, 6, d)
        sh_a, sc_a, g_a, sh_f, sc_f, g_f = [m_lat[:, :, i, :] for i in range(6)]
        m_ctx = jnp.broadcast_to(mod[l, bsz].reshape(1, 1, 6, d), (bsz, 1, 6, d))
        csh_a, csc_a, cg_a, csh_f, csc_f, cg_f = [m_ctx[:, :, i, :] for i in range(6)]

        outs = _inproj_call(x, sh_a, sc_a, w_uqk[l], w_vt[l], ab[l], tabs_lat, prev=prev, alpha=alpha)
        u12, q, k, v = outs[:4]
        if prev is not None:
            x = outs[4]
        outs_c = _inproj_call(xc, csh_a, csc_a, w_uqk[l], w_vt[l], ab[l], tabs_ctx, prev=prev_c, alpha=alpha)
        u12c, qc, kc, vc = outs_c[:4]
        if prev_c is not None:
            xc = outs_c[4]

        o_attn = _attn_call(q, [(kc, vc), (k, v)], lam, subln_g[l], lam_init)
        f = _fourier_call(u12, *dft_lat)
        counts = jnp.zeros((32, LANES), F32)
        rows, routes = [], []
        if not last:
            oc_attn = _attn_call(qc, [(kc, vc)], lam, subln_g[l], lam_init)
            fc = _fourier_call(u12c, *dft_ctx)
            xc, hxc, route_c, counts = _outproj_call(fc, oc_attn, xc, cg_a, ln_attn_g[l], ln_attn_b[l], csh_f,
                                                     csc_f, w_out_b[l], wr_hilo, router_bias, counts, alpha)
            rows.append(hxc.reshape(bsz * l_ctx, -1))
            routes.append(route_c)
        x, hx, route_t, counts = _outproj_call(f, o_attn, x, g_a, ln_attn_g[l], ln_attn_b[l], sh_f, sc_f,
                                               w_out_b[l], wr_hilo, router_bias, counts, alpha)
        rows.append(hx.reshape(bsz * s, -1))
        routes.append(route_t)

        ys = _moe_ffn(rows, routes, counts, w_gate[l], w_up[l], w_down[l])
        if not last:
            prev_c = (ys[0].reshape(bsz, l_ctx, -1), cg_f, ln_ffn_g[l], ln_ffn_b[l])
        prev = (ys[-1].reshape(bsz, s, -1), g_f, ln_ffn_g[l], ln_ffn_b[l])

    y, g_f, lg, lb = prev
    return _ffn_ln_call(x, y, g_f, lg, lb, alpha)
```

```python
import functools
import math

import jax
import jax.numpy as jnp
import numpy as np
from jax import lax
from jax.experimental import pallas as pl
from jax.experimental.pallas import tpu as pltpu

F32 = jnp.float32
BF16 = jnp.bfloat16

GRID_W = 64
FOURIER_WIDTH = 256
FOURIER_GROUPS = 4
FOURIER_GROUP_DIM = FOURIER_WIDTH // FOURIER_GROUPS
HEAD_DIM = 64
V_DIM = 2 * HEAD_DIM
ROPE_BASE = 10000.0
ROPE_AXIS_DIM = HEAD_DIM // 2
N_EXPERTS = 16
N_GROUPS = 4
EXPERTS_PER_GROUP = N_EXPERTS // N_GROUPS
N_PAIR_CLASSES = N_GROUPS * (EXPERTS_PER_GROUP * (EXPERTS_PER_GROUP - 1) // 2)
TOP_K = 2
LN_EPS = 1e-5
LOG2E = math.log2(math.e)

LANES = 128
SUBLANES_BF16 = 16
VMEM_LIMIT = 48 * 1024 * 1024


def _cparams(sem):
    return pltpu.CompilerParams(dimension_semantics=sem, vmem_limit_bytes=VMEM_LIMIT)


def _split_bf16(a):
    hi = a.astype(BF16)
    lo = (a - hi.astype(F32)).astype(BF16)
    return hi, lo


def _dot(a, b):
    return jnp.dot(a, b, preferred_element_type=F32)


def _dot_nt(a, b):
    return lax.dot_general(a, b, (((1,), (1,)), ((), ())), preferred_element_type=F32)


def _dot3(a, w):
    ah, al = _split_bf16(a)
    wh, wl = _split_bf16(w)
    return _dot(ah, wh) + _dot(ah, wl) + _dot(al, wh)


def _ln(x):
    mu = jnp.mean(x, axis=-1, keepdims=True)
    xc = x - mu
    var = jnp.mean(xc * xc, axis=-1, keepdims=True)
    return xc * lax.rsqrt(var + LN_EPS)


def _mod_kernel(c_ref, w_ref, b_ref, o_ref):
    c = c_ref[...]
    s = c * jax.nn.sigmoid(c)
    o_ref[0] = _dot3(s, w_ref[0]) + b_ref[0]


def _mod_call(cc, w_mod, b_mod):
    depth, d, n = w_mod.shape
    r = cc.shape[0]
    tn = 1536
    return pl.pallas_call(
        _mod_kernel,
        out_shape=jax.ShapeDtypeStruct((depth, r, n), F32),
        grid=(depth, n // tn),
        in_specs=[
            pl.BlockSpec((r, d), lambda l, j: (0, 0)),
            pl.BlockSpec((1, d, tn), lambda l, j: (l, 0, j)),
            pl.BlockSpec((1, 1, tn), lambda l, j: (l, 0, j)),
        ],
        out_specs=pl.BlockSpec((1, r, tn), lambda l, j: (l, 0, j)),
        compiler_params=_cparams(("arbitrary", "arbitrary")),
        name="mod",
    )(cc, w_mod, b_mod.reshape(depth, 1, n))


def _fprep_kernel(c_ref, s_ref, w_ref, a_ref, b_ref):
    w = w_ref[0]
    a_ref[0] = _dot3(c_ref[...], w)
    b_ref[0] = _dot3(s_ref[...], w)


def _fprep_call(w_fourier):
    depth, g, c, _ = w_fourier.shape
    idx = np.arange(c)
    ang = 2.0 * np.pi * ((idx[:, None] * idx[None, :]) % c) / c
    c64 = jnp.asarray(np.cos(ang), F32)
    s64 = jnp.asarray(np.sin(ang), F32)
    wf = w_fourier.reshape(depth * g, c, c)
    spec = pl.BlockSpec((1, c, c), lambda i: (i, 0, 0))
    cst = pl.BlockSpec((c, c), lambda i: (0, 0))
    a, b = pl.pallas_call(
        _fprep_kernel,
        out_shape=(jax.ShapeDtypeStruct(wf.shape, F32),) * 2,
        grid=(depth * g,),
        in_specs=[cst, cst, spec],
        out_specs=(spec, spec),
        compiler_params=_cparams(("arbitrary",)),
        name="fourier_prep",
    )(c64, s64, wf)
    eye = jnp.eye(g, dtype=F32)

    def bd(m):
        m = m.reshape(depth, g, c, c)
        return (m[:, :, :, None, :] * eye[None, :, None, :, None]).reshape(depth, g * c, g * c)

    return jnp.concatenate([bd(a), bd(b)], axis=-1).astype(BF16)


def _inproj_kernel(*refs, fuse_prev, alpha):
    if fuse_prev:
        (x_ref, y_ref, gf_ref, lg_ref, lb_ref, sh_ref, sc_ref, w_ref, wvt_ref, ab_ref,
         cq_ref, sq_ref, ck_ref, sk_ref, u_ref, q_ref, k_ref, v_ref, xo_ref) = refs
        z = alpha * x_ref[0] + gf_ref[0] * _unpack_halves(y_ref[0])
        x = _ln(z) * lg_ref[...] + lb_ref[...]
        xo_ref[0] = x
    else:
        (x_ref, sh_ref, sc_ref, w_ref, wvt_ref, ab_ref,
         cq_ref, sq_ref, ck_ref, sk_ref, u_ref, q_ref, k_ref, v_ref) = refs
        x = x_ref[0]
    h = (_ln(x) * (1.0 + sc_ref[0]) + sh_ref[0]).astype(BF16)

    fw = FOURIER_WIDTH
    aw = (w_ref.shape[1] - fw) // 2
    nh = aw // LANES
    u = _dot(h, w_ref[:, :fw]).astype(BF16)
    u_ref[0] = _dot(u, ab_ref[...]).astype(BF16)

    tm = h.shape[0]
    lane = lax.broadcasted_iota(jnp.int32, (tm, LANES), 1)
    half = ROPE_AXIS_DIM // 2
    first_half = (lane & half) == 0

    def rope_store(p, cos, sin, dst_ref):
        for j in range(nh):
            pj = p[:, j * LANES:(j + 1) * LANES]
            swapped = jnp.where(first_half, pltpu.roll(pj, LANES - half, 1), pltpu.roll(pj, half, 1))
            dst_ref[0, j] = (pj * cos + swapped * sin).astype(BF16)

    rope_store(_dot(h, w_ref[:, fw:fw + aw]), cq_ref[...], sq_ref[...], q_ref)
    rope_store(_dot(h, w_ref[:, fw + aw:]), ck_ref[...], sk_ref[...], k_ref)
    vt = _dot_nt(wvt_ref[...], h)
    for j in range(nh):
        v_ref[0, j] = vt[j * V_DIM:(j + 1) * V_DIM, :].astype(BF16)


def _inproj_call(x, shift, scale, w_uqk, w_vt, ab, tabs, prev=None, alpha=1.0, tm=512):
    bsz, t, d = x.shape
    aw = w_vt.shape[0]
    nh = aw // V_DIM
    tm = min(tm, t)
    grid = (bsz, t // tm)
    row = pl.BlockSpec((1, tm, d), lambda b, i: (b, i, 0))
    vec = pl.BlockSpec((1, 1, d), lambda b, i: (b, 0, 0))
    cst = lambda shp: pl.BlockSpec(shp, lambda b, i: (0,) * len(shp))
    tab = pl.BlockSpec((tm, LANES), lambda b, i: (i, 0))
    in_specs, args = [row], [x]
    if prev is not None:
        y, gf, lg, lb = prev
        in_specs += [pl.BlockSpec((1, tm, d // 2), lambda b, i: (b, i, 0)), vec, cst((1, d)), cst((1, d))]
        args += [y, gf, lg.reshape(1, d), lb.reshape(1, d)]
    in_specs += [vec, vec, cst(w_uqk.shape), cst(w_vt.shape), cst(ab.shape), tab, tab, tab, tab]
    args += [shift, scale, w_uqk, w_vt, ab, *tabs]
    head_rows = pl.BlockSpec((1, nh, tm, V_DIM), lambda b, i: (b, 0, i, 0))
    out_shape = [
        jax.ShapeDtypeStruct((bsz, t, 2 * FOURIER_WIDTH), BF16),
        jax.ShapeDtypeStruct((bsz, nh, t, V_DIM), BF16),
        jax.ShapeDtypeStruct((bsz, nh, t, V_DIM), BF16),
        jax.ShapeDtypeStruct((bsz, nh, V_DIM, t), BF16),
    ]
    out_specs = [
        pl.BlockSpec((1, tm, 2 * FOURIER_WIDTH), lambda b, i: (b, i, 0)),
        head_rows,
        head_rows,
        pl.BlockSpec((1, nh, V_DIM, tm), lambda b, i: (b, 0, 0, i)),
    ]
    if prev is not None:
        out_shape.append(jax.ShapeDtypeStruct((bsz, t, d), F32))
        out_specs.append(row)
    return pl.pallas_call(
        functools.partial(_inproj_kernel, fuse_prev=prev is not None, alpha=alpha),
        out_shape=tuple(out_shape),
        grid=grid,
        in_specs=in_specs,
        out_specs=tuple(out_specs),
        compiler_params=_cparams(("arbitrary", "arbitrary")),
        name="inproj",
    )(*args)


def _attn_kernel(lam_ref, q_ref, *refs, n_seg, norm_scale, tq):
    kv_refs = refs[:2 * n_seg]
    g_ref, o_ref, s_buf, m_buf = refs[2 * n_seg:]
    nh, t = q_ref.shape[1], q_ref.shape[2]
    n_iter = (t // tq) * nh
    lane = lax.broadcasted_iota(jnp.int32, (tq, LANES), 1)
    lam = lam_ref[0]
    gain = g_ref[...] * norm_scale
    seg_len = [kv_refs[2 * j].shape[2] for j in range(n_seg)]
    seg_off = [sum(seg_len[:j]) for j in range(n_seg)]

    def where(it):
        h = lax.rem(it, nh)
        rows = pl.ds(pl.multiple_of(lax.div(it, nh) * tq, tq), tq)
        return h, rows

    def scores(it, sub):
        h, rows = where(it)
        qb = q_ref[0, h, rows, :]
        qs = jnp.where((lane < HEAD_DIM) == (sub == 0), qb, jnp.zeros_like(qb))
        m = None
        for j in range(n_seg):
            s = _dot_nt(kv_refs[2 * j][0, h], qs)
            s_buf[sub, seg_off[j]:seg_off[j] + seg_len[j], :] = s
            mj = jnp.max(s, axis=0, keepdims=True)
            m = mj if m is None else jnp.maximum(m, mj)
        m_buf[sub] = jnp.broadcast_to(m, m_buf.shape[1:])

    def attend(it, sub):
        h, _ = where(it)
        m = m_buf[sub][0:1]
        acc = None
        for j in range(n_seg):
            p = jnp.exp2(s_buf[sub, seg_off[j]:seg_off[j] + seg_len[j], :] - m).astype(BF16)
            vt = kv_refs[2 * j + 1][0, h]
            vext = jnp.concatenate([vt, jnp.ones((SUBLANES_BF16, vt.shape[1]), BF16)], axis=0)
            aj = _dot(vext, p)
            acc = aj if acc is None else acc + aj
        return acc[:V_DIM] / acc[V_DIM:V_DIM + 1]

    def finish(it, o0, o1):
        h, rows = where(it)
        o = o0 - lam * o1
        o = o * lax.rsqrt(jnp.mean(o * o, axis=0, keepdims=True) + LN_EPS)
        o_ref[0, h, rows, :] = (o.T * gain).astype(BF16)

    scores(0, 0)

    def step(it, carry):
        scores(it, 1)
        o0 = attend(it, 0)
        scores(it + 1, 0)
        o1 = attend(it, 1)
        finish(it, o0, o1)
        return carry

    lax.fori_loop(0, n_iter - 1, step, 0)
    last = n_iter - 1
    scores(last, 1)
    o0 = attend(last, 0)
    o1 = attend(last, 1)
    finish(last, o0, o1)


def _attn_call(q, kv_segs, lam, g, lam_init, tq=512):
    bsz, nh, t, _ = q.shape
    tq = min(tq, t)
    whole = lambda a: pl.BlockSpec((1,) + a.shape[1:], lambda b: (b, 0, 0, 0))
    in_specs = [pl.BlockSpec(memory_space=pltpu.SMEM), whole(q)]
    args = [lam.reshape(1).astype(F32), q]
    for k, vt in kv_segs:
        in_specs += [whole(k), whole(vt)]
        args += [k, vt]
    in_specs.append(pl.BlockSpec((1, V_DIM), lambda b: (0, 0)))
    args.append(g.reshape(1, V_DIM).astype(F32))
    n_keys = sum(k.shape[2] for k, _ in kv_segs)
    return pl.pallas_call(
        functools.partial(_attn_kernel, n_seg=len(kv_segs), norm_scale=1.0 - lam_init, tq=tq),
        out_shape=jax.ShapeDtypeStruct(q.shape, BF16),
        grid=(bsz,),
        in_specs=in_specs,
        out_specs=whole(q),
        scratch_shapes=[pltpu.VMEM((2, n_keys, tq), F32), pltpu.VMEM((2, 8, tq), F32)],
        compiler_params=_cparams(("arbitrary",)),
        name="diff_attn",
    )(*args)


def _fourier_kernel(c_ref, s_ref, u1_ref, u2_ref, o_ref):
    o_ref[0] = (_dot(c_ref[...], u1_ref[0]) - _dot(s_ref[...], u2_ref[0])).astype(BF16)


def _dft_mats(t):
    idx = jnp.arange(t, dtype=jnp.int32)
    ang = ((idx[:, None] * idx[None, :]) % t).astype(F32) * (2.0 * math.pi / t)
    scale = 1.0 / math.sqrt(t * FOURIER_GROUP_DIM)
    return (jnp.cos(ang) * scale).astype(BF16), (jnp.sin(ang) * scale).astype(BF16)


def _fourier_call(u12, cmat, smat, tk=1024):
    bsz, t, w2 = u12.shape
    w = w2 // 2
    tk = min(tk, t)
    return pl.pallas_call(
        _fourier_kernel,
        out_shape=jax.ShapeDtypeStruct((bsz, t, w), BF16),
        grid=(t // tk, bsz),
        in_specs=[
            pl.BlockSpec((tk, t), lambda i, b: (i, 0)),
            pl.BlockSpec((tk, t), lambda i, b: (i, 0)),
            pl.BlockSpec((1, t, w), lambda i, b: (b, 0, 0)),
            pl.BlockSpec((1, t, w), lambda i, b: (b, 0, 1)),
        ],
        out_specs=pl.BlockSpec((1, tk, w), lambda i, b: (b, i, 0)),
        compiler_params=_cparams(("arbitrary", "arbitrary")),
        name="fourier_dft",
    )(cmat, smat, u12, u12)


def _pack_halves(v):
    n = v.shape[1] // 2
    bits = lax.bitcast_convert_type(v, jnp.uint32)
    return bits[:, :n] | (bits[:, n:] >> 16)


def _unpack_halves(u):
    hi = lax.bitcast_convert_type(u & jnp.uint32(0xFFFF0000), F32)
    lo = lax.bitcast_convert_type(u << 16, F32)
    return jnp.concatenate([hi, lo], axis=1)


def _route_rows(logits, bias):
    score = jax.nn.sigmoid(logits)
    sel = score + bias
    r = [sel[j:j + 1, :] for j in range(N_EXPERTS)]
    s = [score[j:j + 1, :] for j in range(N_EXPERTS)]
    npg = EXPERTS_PER_GROUP
    best = None
    for g in range(N_GROUPS):
        v = r[g * npg:(g + 1) * npg]
        pair_sums = [v[a] + v[b] for a in range(npg) for b in range(a + 1, npg)]
        tg = functools.reduce(jnp.maximum, pair_sums)
        if best is None:
            best, bg = tg, jnp.zeros_like(tg)
        else:
            upd = tg > best
            best = jnp.where(upd, tg, best)
            bg = jnp.where(upd, float(g), bg)

    def of_group(rows, j):
        out = rows[j]
        for g in range(1, N_GROUPS):
            out = jnp.where(bg == float(g), rows[g * npg + j], out)
        return out

    v = [of_group(r, j) for j in range(npg)]
    sv = [of_group(s, j) for j in range(npg)]

    def first_argmax(vals):
        m = functools.reduce(jnp.maximum, vals)
        idx = jnp.full_like(m, float(npg - 1))
        for j in range(npg - 2, -1, -1):
            idx = jnp.where(vals[j] == m, float(j), idx)
        return idx

    i1 = first_argmax(v)
    i2 = first_argmax([jnp.where(i1 == float(j), -jnp.inf, v[j]) for j in range(npg)])
    lo = jnp.minimum(i1, i2)
    hi = jnp.maximum(i1, i2)

    def pick(vals, idx):
        out = vals[0]
        for j in range(1, npg):
            out = jnp.where(idx == float(j), vals[j], out)
        return out

    s_lo, s_hi = pick(sv, lo), pick(sv, hi)
    den = s_lo + s_hi
    pair = lo * (2.0 * npg - 1.0 - lo) * 0.5 + (hi - lo - 1.0)
    cls = bg * float(N_PAIR_CLASSES // N_GROUPS) + pair
    return cls, s_lo / den, s_hi / den


def _outproj_kernel(f_ref, o_ref, x_ref, ga_ref, lg_ref, lb_ref, sh_ref, sc_ref, w_ref, wr_ref, rb_ref, tri_ref,
                    cin_ref, xo_ref, hx_ref, rt_ref, cnt_ref, *, alpha):
    first = (pl.program_id(0) == 0) & (pl.program_id(1) == 0)

    @pl.when(first)
    def _():
        cnt_ref[...] = cin_ref[...]

    fw = f_ref.shape[2]
    o = jnp.concatenate([o_ref[0, j] for j in range(o_ref.shape[1])], axis=1)
    proj = _dot(f_ref[0], w_ref[:fw, :]) + _dot(o, w_ref[fw:, :])
    z = alpha * x_ref[0] + ga_ref[0] * proj
    x = _ln(z) * lg_ref[...] + lb_ref[...]
    xo_ref[0] = x
    h = _ln(x) * (1.0 + sc_ref[0]) + sh_ref[0]
    hh, hl = _split_bf16(h)
    r = _dot_nt(wr_ref[...], hh)
    r2 = _dot_nt(wr_ref[:N_EXPERTS, :], hl)
    logits = r[:N_EXPERTS] + r[N_EXPERTS:] + r2
    cls, w0, w1 = _route_rows(logits, rb_ref[...])

    tm = h.shape[0]
    ncls = cnt_ref.shape[0]
    cls_iota = lax.broadcasted_iota(jnp.int32, (ncls, tm), 0).astype(F32)
    onehot = cls_iota == cls
    oh = onehot.astype(F32)
    prefix = _dot(oh.astype(BF16), tri_ref[...])
    counts = cnt_ref[...]
    base = jnp.concatenate([counts] * (tm // LANES), axis=1)
    rank = jnp.sum(jnp.where(onehot, prefix + base, 0.0), axis=0, keepdims=True)
    cnt_ref[...] = counts + jnp.sum(oh, axis=1, keepdims=True)

    row_iota = lax.broadcasted_iota(jnp.int32, (rt_ref.shape[0], tm), 0)
    rt_ref[...] = jnp.where(row_iota == 0, cls, jnp.where(row_iota == 1, rank, 0.0))

    wrow_iota = lax.broadcasted_iota(jnp.int32, (LANES, tm), 0)
    wcols = jnp.where(wrow_iota == 0, w0, jnp.where(wrow_iota == 1, w1, 0.0)).T
    hx_ref[0] = jnp.concatenate([_pack_halves(hh.astype(F32)), lax.bitcast_convert_type(wcols, jnp.uint32)],
                                axis=1)


def _outproj_call(f, o, x, g_a, ln_g, ln_b, shift, scale, w_out, wr_hilo, router_bias, counts_in, alpha, tm=512):
    bsz, t, d = x.shape
    tm = min(tm, t)
    nt = t // tm
    row = lambda w: pl.BlockSpec((1, tm, w), lambda b, i: (b, i, 0))
    vec = pl.BlockSpec((1, 1, d), lambda b, i: (b, 0, 0))
    cst = lambda shp: pl.BlockSpec(shp, lambda b, i: (0,) * len(shp))
    tri = jnp.asarray(np.triu(np.ones((tm, tm), np.float32), 1), BF16)
    wrow = d // 2 + LANES
    return pl.pallas_call(
        functools.partial(_outproj_kernel, alpha=alpha),
        out_shape=(
            jax.ShapeDtypeStruct((bsz, t, d), F32),
            jax.ShapeDtypeStruct((bsz, t, wrow), jnp.uint32),
            jax.ShapeDtypeStruct((8, bsz * t), F32),
            jax.ShapeDtypeStruct(counts_in.shape, F32),
        ),
        grid=(bsz, nt),
        in_specs=[row(f.shape[2]), pl.BlockSpec((1, o.shape[1], tm, V_DIM), lambda b, i: (b, 0, i, 0)), row(d),
                  vec, cst((1, d)), cst((1, d)), vec, vec, cst(w_out.shape), cst(wr_hilo.shape),
                  cst((N_EXPERTS, 1)), cst((tm, tm)), cst(counts_in.shape)],
        out_specs=(row(d), row(wrow), pl.BlockSpec((8, tm), lambda b, i: (0, b * nt + i)),
                   cst(counts_in.shape)),
        compiler_params=_cparams(("arbitrary", "arbitrary")),
        name="outproj",
    )(f, o, x, g_a, ln_g.reshape(1, d), ln_b.reshape(1, d), shift, scale, w_out, wr_hilo,
      router_bias.reshape(N_EXPERTS, 1).astype(F32), tri, counts_in)


ROW_GROUP = 8


def _row_copies(n_groups, src_row, dst_row, sem):
    def start(g, carry):
        for j in range(ROW_GROUP):
            pltpu.make_async_copy(src_row(g, j), dst_row(g, j), sem).start(priority=j % 2)
        return carry

    def wait(g, carry):
        for j in range(ROW_GROUP):
            pltpu.make_async_copy(src_row(0, 0), dst_row(0, 0), sem).wait()
        return carry

    lax.fori_loop(0, n_groups, start, 0)
    lax.fori_loop(0, n_groups, wait, 0)


def _scatter_rows_kernel(pos_ref, src_ref, dst_in_ref, dst_ref, sem):
    del dst_in_ref
    _row_copies(src_ref.shape[0],
                lambda g, j: src_ref.at[g, pl.ds(j, 1), :],
                lambda g, j: dst_ref.at[pl.ds(pos_ref[0, 0, g * ROW_GROUP + j], 1), :], sem)


def _scatter_rows_call(src, pos, dst, rows_per_step=512):
    t, w = src.shape
    rs = min(rows_per_step, t)
    return pl.pallas_call(
        _scatter_rows_kernel,
        out_shape=jax.ShapeDtypeStruct(dst.shape, dst.dtype),
        grid=(t // rs,),
        in_specs=[pl.BlockSpec((1, 1, rs), lambda i: (i, 0, 0), memory_space=pltpu.SMEM),
                  pl.BlockSpec((rs // ROW_GROUP, ROW_GROUP, w), lambda i: (i, 0, 0)),
                  pl.BlockSpec(memory_space=pl.ANY)],
        out_specs=pl.BlockSpec(memory_space=pl.ANY),
        scratch_shapes=[pltpu.SemaphoreType.DMA(())],
        input_output_aliases={2: 0},
        compiler_params=_cparams(("arbitrary",)),
        name="scatter_rows",
    )(pos.reshape(t // rs, 1, rs), src.reshape(t // ROW_GROUP, ROW_GROUP, w), dst)


def _gather_rows_kernel(pos_ref, src_ref, dst_ref, sem):
    _row_copies(dst_ref.shape[0],
                lambda g, j: src_ref.at[pl.ds(pos_ref[0, 0, g * ROW_GROUP + j], 1), :],
                lambda g, j: dst_ref.at[g, pl.ds(j, 1), :], sem)


def _gather_rows_call(src, pos, rows_per_step=512):
    t = pos.shape[0]
    w = src.shape[1]
    rs = min(rows_per_step, t)
    out = pl.pallas_call(
        _gather_rows_kernel,
        out_shape=jax.ShapeDtypeStruct((t // ROW_GROUP, ROW_GROUP, w), src.dtype),
        grid=(t // rs,),
        in_specs=[pl.BlockSpec((1, 1, rs), lambda i: (i, 0, 0), memory_space=pltpu.SMEM),
                  pl.BlockSpec(memory_space=pl.ANY)],
        out_specs=pl.BlockSpec((rs // ROW_GROUP, ROW_GROUP, w), lambda i: (i, 0, 0)),
        scratch_shapes=[pltpu.SemaphoreType.DMA(())],
        compiler_params=_cparams(("arbitrary",)),
        name="gather_rows",
    )(pos.reshape(t // rs, 1, rs), src)
    return out.reshape(t, w)


def _moe_kernel(e0_ref, e1_ref, valid_ref, fresh_ref, x_ref, wg0, wu0, wd0, wg1, wu1, wd1, y_ref,
                wg_s, wu_s, wd_s):
    i = pl.program_id(0)
    half = wg_s.shape[1] // 2

    @pl.when(fresh_ref[i] != 0)
    def _():
        for slot, (wg, wu, wd) in enumerate(((wg0, wu0, wd0), (wg1, wu1, wd1))):
            wg_s[slot] = wg[0].astype(BF16)
            wu_s[slot] = wu[0].astype(BF16)
            wd_s[slot] = wd[0].astype(BF16)

    @pl.when(valid_ref[i] != 0)
    def _():
        xrow = x_ref[...]
        x = _unpack_halves(xrow[:, :half]).astype(BF16)
        wt = lax.bitcast_convert_type(xrow[:, half:], F32)

        def expert(slot):
            g = _dot(x, wg_s[slot])
            u = _dot(x, wu_s[slot])
            a = (g * jax.nn.sigmoid(g) * u).astype(BF16)
            return _dot(a, wd_s[slot])

        y = wt[:, 0:1] * expert(0) + wt[:, 1:2] * expert(1)
        y_ref[...] = _pack_halves(y.astype(BF16).astype(F32))

    @pl.when(valid_ref[i] == 0)
    def _():
        y_ref[...] = jnp.zeros_like(y_ref)


def _moe_call(xs, tile_e0, tile_e1, tile_valid, tile_fresh, w_gate, w_up, w_down, tmm):
    tpad, wrow = xs.shape
    _, d, de = w_gate.shape
    n_tiles = tpad // tmm
    wspec0 = lambda shp: pl.BlockSpec((1,) + shp, lambda i, e0, e1, vl, fr: (e0[i], 0, 0))
    wspec1 = lambda shp: pl.BlockSpec((1,) + shp, lambda i, e0, e1, vl, fr: (e1[i], 0, 0))
    grid_spec = pltpu.PrefetchScalarGridSpec(
        num_scalar_prefetch=4,
        grid=(n_tiles,),
        in_specs=[
            pl.BlockSpec((tmm, wrow), lambda i, e0, e1, vl, fr: (i, 0)),
            wspec0((d, de)), wspec0((d, de)), wspec0((de, d)),
            wspec1((d, de)), wspec1((d, de)), wspec1((de, d)),
        ],
        out_specs=pl.BlockSpec((tmm, d // 2), lambda i, e0, e1, vl, fr: (i, 0)),
        scratch_shapes=[pltpu.VMEM((2, d, de), BF16), pltpu.VMEM((2, d, de), BF16), pltpu.VMEM((2, de, d), BF16)],
    )
    return pl.pallas_call(
        _moe_kernel,
        out_shape=jax.ShapeDtypeStruct((tpad, d // 2), jnp.uint32),
        grid_spec=grid_spec,
        compiler_params=_cparams(("arbitrary",)),
        name="moe_ffn",
    )(tile_e0, tile_e1, tile_valid, tile_fresh, xs, w_gate, w_up, w_down, w_gate, w_up, w_down)


def _pair_tables():
    e0s, e1s = [], []
    for g in range(N_GROUPS):
        for a in range(EXPERTS_PER_GROUP):
            for b in range(a + 1, EXPERTS_PER_GROUP):
                e0s.append(g * EXPERTS_PER_GROUP + a)
                e1s.append(g * EXPERTS_PER_GROUP + b)
    return np.asarray(e0s, np.int32), np.asarray(e1s, np.int32)


def _moe_ffn(rows, routes, counts, w_gate, w_up, w_down, tmm=512):
    t_all = sum(r.shape[0] for r in rows)
    cnt = counts[:N_PAIR_CLASSES, 0].astype(jnp.int32)
    padded = ((cnt + tmm - 1) // tmm) * tmm
    ends = jnp.cumsum(padded)
    starts = ends - padded
    n_tiles = -(-t_all // tmm) + N_PAIR_CLASSES
    tpad = n_tiles * tmm
    total = ends[-1]
    tile_start = jnp.arange(n_tiles, dtype=jnp.int32) * tmm
    tile_valid = (tile_start < total).astype(jnp.int32)
    probe = jnp.minimum(tile_start, total - 1)
    tile_cls = jnp.sum((ends[None, :] <= probe[:, None]).astype(jnp.int32), axis=1)
    tile_cls = jnp.minimum(tile_cls, N_PAIR_CLASSES - 1)
    pe0, pe1 = _pair_tables()
    tile_e0 = jnp.asarray(pe0)[tile_cls]
    tile_e1 = jnp.asarray(pe1)[tile_cls]
    tile_fresh = jnp.concatenate([jnp.ones((1,), jnp.int32),
                                  (tile_cls[1:] != tile_cls[:-1]).astype(jnp.int32)])
    cls_ids = jnp.arange(N_PAIR_CLASSES, dtype=F32)[:, None]
    starts_f = starts.astype(F32)[:, None]
    poss = []
    for rt in routes:
        start_of_tok = jnp.sum(jnp.where(rt[0][None, :] == cls_ids, starts_f, 0.0), axis=0)
        poss.append((start_of_tok + rt[1]).astype(jnp.int32))
    xs = jnp.zeros((tpad, rows[0].shape[1]), jnp.uint32)
    for r, pos in zip(rows, poss):
        xs = _scatter_rows_call(r, pos, xs)
    ys = _moe_call(xs, tile_e0, tile_e1, tile_valid, tile_fresh, w_gate, w_up, w_down, tmm)
    return [_gather_rows_call(ys, pos) for pos in poss]


def _ffn_ln_kernel(x_ref, y_ref, gf_ref, lg_ref, lb_ref, o_ref, *, alpha):
    z = alpha * x_ref[0] + gf_ref[0] * _unpack_halves(y_ref[0])
    o_ref[0] = _ln(z) * lg_ref[...] + lb_ref[...]


def _ffn_ln_call(x, y, g_f, ln_g, ln_b, alpha, tm=512):
    bsz, t, d = x.shape
    tm = min(tm, t)
    row = pl.BlockSpec((1, tm, d), lambda b, i: (b, i, 0))
    vec = pl.BlockSpec((1, 1, d), lambda b, i: (b, 0, 0))
    cst = pl.BlockSpec((1, d), lambda b, i: (0, 0))
    return pl.pallas_call(
        functools.partial(_ffn_ln_kernel, alpha=alpha),
        out_shape=jax.ShapeDtypeStruct((bsz, t, d), F32),
        grid=(bsz, t // tm),
        in_specs=[row, pl.BlockSpec((1, tm, d // 2), lambda b, i: (b, i, 0)), vec, cst, cst],
        out_specs=row,
        compiler_params=_cparams(("arbitrary", "arbitrary")),
        name="ffn_ln",
    )(x, y, g_f, ln_g.reshape(1, d), ln_b.reshape(1, d))


def _rope_tables(s):
    rows = s // GRID_W
    row = jnp.repeat(jnp.arange(rows), GRID_W).astype(F32)
    col = jnp.tile(jnp.arange(GRID_W), rows).astype(F32)
    freqs = ROPE_BASE ** (-jnp.arange(0, ROPE_AXIS_DIM, 2, dtype=F32) / ROPE_AXIS_DIM)
    ang_row = row[:, None] * freqs
    ang_col = col[:, None] * freqs

    def head(r, c, sign):
        return jnp.concatenate([sign * r, r, sign * c, c], axis=1)

    cos = head(jnp.cos(ang_row), jnp.cos(ang_col), 1.0)
    sin = head(jnp.sin(ang_row), jnp.sin(ang_col), -1.0)
    cos = jnp.concatenate([cos, cos], axis=1)
    sin = jnp.concatenate([sin, sin], axis=1)
    return cos, sin


def kernel(x, c, ctx, c_ctx, w_mod, b_mod, w_in, w_fourier, lam_qk, subln_g, w_out, ln_attn_g, ln_attn_b,
           ln_ffn_g, ln_ffn_b, w_router, router_bias, w_gate, w_up, w_down):
    bsz, s, d = x.shape
    l_ctx = ctx.shape[1]
    depth = w_mod.shape[0]
    alpha = (2 * depth) ** 0.25
    qscale = LOG2E * HEAD_DIM ** -0.5

    pad = (-(bsz + 1)) % 8
    cc = jnp.concatenate([c, c_ctx[None, :], jnp.zeros((pad, d), F32)], axis=0)
    mod = _mod_call(cc, w_mod, b_mod)

    ab = _fprep_call(w_fourier)
    v_off = w_in.shape[2] - (w_in.shape[2] - FOURIER_WIDTH) // 3
    w_uqk = w_in[:, :, :v_off].astype(BF16)
    w_vt = jnp.swapaxes(w_in[:, :, v_off:], 1, 2).astype(BF16)
    w_out_b = w_out.astype(BF16)
    wr_t = w_router.T.astype(F32)
    wr_hi = wr_t.astype(BF16)
    wr_lo = (wr_t - wr_hi.astype(F32)).astype(BF16)
    wr_hilo = jnp.concatenate([wr_hi, wr_lo], axis=0)

    cos, sin = _rope_tables(s)
    tabs_lat = (cos * qscale, sin * qscale, cos, sin)
    ones = jnp.ones((l_ctx, LANES), F32)
    zeros = jnp.zeros((l_ctx, LANES), F32)
    tabs_ctx = (ones * qscale, zeros, ones, zeros)
    dft_lat = _dft_mats(s)
    dft_ctx = _dft_mats(l_ctx)

    xc = ctx
    prev = None
    prev_c = None
    for l in range(depth):
        last = l == depth - 1
        lam_init = 0.8 - 0.6 * math.exp(-0.3 * l)
        lq = lam_qk[l].astype(F32)
        lam = jnp.exp(jnp.sum(lq[0] * lq[1])) - jnp.exp(jnp.sum(lq[2] * lq[3])) + lam_init

        m_lat = mod[l, :bsz].reshape(bsz, 1, 6, d)
        sh_a, sc_a, g_a, sh_f, sc_f, g_f = [m_lat[:, :, i, :] for i in range(6)]
        m_ctx = jnp.broadcast_to(mod[l, bsz].reshape(1, 1, 6, d), (bsz, 1, 6, d))
        csh_a, csc_a, cg_a, csh_f, csc_f, cg_f = [m_ctx[:, :, i, :] for i in range(6)]

        outs = _inproj_call(x, sh_a, sc_a, w_uqk[l], w_vt[l], ab[l], tabs_lat, prev=prev, alpha=alpha)
        u12, q, k, v = outs[:4]
        if prev is not None:
            x = outs[4]
        outs_c = _inproj_call(xc, csh_a, csc_a, w_uqk[l], w_vt[l], ab[l], tabs_ctx, prev=prev_c, alpha=alpha)
        u12c, qc, kc, vc = outs_c[:4]
        if prev_c is not None:
            xc = outs_c[4]

        o_attn = _attn_call(q, [(kc, vc), (k, v)], lam, subln_g[l], lam_init)
        f = _fourier_call(u12, *dft_lat)
        counts = jnp.zeros((32, LANES), F32)
        rows, routes = [], []
        if not last:
            oc_attn = _attn_call(qc, [(kc, vc)], lam, subln_g[l], lam_init)
            fc = _fourier_call(u12c, *dft_ctx)
            xc, hxc, route_c, counts = _outproj_call(fc, oc_attn, xc, cg_a, ln_attn_g[l], ln_attn_b[l], csh_f,
                                                     csc_f, w_out_b[l], wr_hilo, router_bias, counts, alpha)
            rows.append(hxc.reshape(bsz * l_ctx, -1))
            routes.append(route_c)
        x, hx, route_t, counts = _outproj_call(f, o_attn, x, g_a, ln_attn_g[l], ln_attn_b[l], sh_f, sc_f,
                                               w_out_b[l], wr_hilo, router_bias, counts, alpha)
        rows.append(hx.reshape(bsz * s, -1))
        routes.append(route_t)

        ys = _moe_ffn(rows, routes, counts, w_gate[l], w_up[l], w_down[l])
        if not last:
            prev_c = (ys[0].reshape(bsz, l_ctx, -1), cg_f, ln_ffn_g[l], ln_ffn_b[l])
        prev = (ys[-1].reshape(bsz, s, -1), g_f, ln_ffn_g[l], ln_ffn_b[l])

    y, g_f, lg, lb = prev
    return _ffn_ln_call(x, y, g_f, lg, lb, alpha)
```

```python
import functools
import math

import jax
import jax.numpy as jnp
import numpy as np
from jax import lax
from jax.experimental import pallas as pl
from jax.experimental.pallas import tpu as pltpu

F32 = jnp.float32
BF16 = jnp.bfloat16

GRID_W = 64
FOURIER_WIDTH = 256
FOURIER_GROUPS = 4
FOURIER_GROUP_DIM = FOURIER_WIDTH // FOURIER_GROUPS
HEAD_DIM = 64
V_DIM = 2 * HEAD_DIM
ROPE_BASE = 10000.0
ROPE_AXIS_DIM = HEAD_DIM // 2
N_EXPERTS = 16
N_GROUPS = 4
EXPERTS_PER_GROUP = N_EXPERTS // N_GROUPS
N_PAIR_CLASSES = N_GROUPS * (EXPERTS_PER_GROUP * (EXPERTS_PER_GROUP - 1) // 2)
TOP_K = 2
LN_EPS = 1e-5
LOG2E = math.log2(math.e)

LANES = 128
SUBLANES_BF16 = 16
VMEM_LIMIT = 48 * 1024 * 1024


def _cparams(sem):
    return pltpu.CompilerParams(dimension_semantics=sem, vmem_limit_bytes=VMEM_LIMIT)


def _split_bf16(a):
    hi = a.astype(BF16)
    lo = (a - hi.astype(F32)).astype(BF16)
    return hi, lo


def _dot(a, b):
    return jnp.dot(a, b, preferred_element_type=F32)


def _dot_nt(a, b):
    return lax.dot_general(a, b, (((1,), (1,)), ((), ())), preferred_element_type=F32)


def _dot3(a, w):
    ah, al = _split_bf16(a)
    wh, wl = _split_bf16(w)
    return _dot(ah, wh) + _dot(ah, wl) + _dot(al, wh)


def _ln(x):
    mu = jnp.mean(x, axis=-1, keepdims=True)
    xc = x - mu
    var = jnp.mean(xc * xc, axis=-1, keepdims=True)
    return xc * lax.rsqrt(var + LN_EPS)


def _mod_kernel(c_ref, w_ref, b_ref, o_ref):
    c = c_ref[...]
    s = c * jax.nn.sigmoid(c)
    o_ref[0] = _dot3(s, w_ref[0]) + b_ref[0]


def _mod_call(cc, w_mod, b_mod):
    depth, d, n = w_mod.shape
    r = cc.shape[0]
    tn = 1536
    return pl.pallas_call(
        _mod_kernel,
        out_shape=jax.ShapeDtypeStruct((depth, r, n), F32),
        grid=(depth, n // tn),
        in_specs=[
            pl.BlockSpec((r, d), lambda l, j: (0, 0)),
            pl.BlockSpec((1, d, tn), lambda l, j: (l, 0, j)),
            pl.BlockSpec((1, 1, tn), lambda l, j: (l, 0, j)),
        ],
        out_specs=pl.BlockSpec((1, r, tn), lambda l, j: (l, 0, j)),
        compiler_params=_cparams(("arbitrary", "arbitrary")),
        name="mod",
    )(cc, w_mod, b_mod.reshape(depth, 1, n))


def _fprep_kernel(c_ref, s_ref, w_ref, a_ref, b_ref):
    w = w_ref[0]
    a_ref[0] = _dot3(c_ref[...], w)
    b_ref[0] = _dot3(s_ref[...], w)


def _fprep_call(w_fourier):
    depth, g, c, _ = w_fourier.shape
    idx = np.arange(c)
    ang = 2.0 * np.pi * ((idx[:, None] * idx[None, :]) % c) / c
    c64 = jnp.asarray(np.cos(ang), F32)
    s64 = jnp.asarray(np.sin(ang), F32)
    wf = w_fourier.reshape(depth * g, c, c)
    spec = pl.BlockSpec((1, c, c), lambda i: (i, 0, 0))
    cst = pl.BlockSpec((c, c), lambda i: (0, 0))
    a, b = pl.pallas_call(
        _fprep_kernel,
        out_shape=(jax.ShapeDtypeStruct(wf.shape, F32),) * 2,
        grid=(depth * g,),
        in_specs=[cst, cst, spec],
        out_specs=(spec, spec),
        compiler_params=_cparams(("arbitrary",)),
        name="fourier_prep",
    )(c64, s64, wf)
    eye = jnp.eye(g, dtype=F32)

    def bd(m):
        m = m.reshape(depth, g, c, c)
        return (m[:, :, :, None, :] * eye[None, :, None, :, None]).reshape(depth, g * c, g * c)

    return jnp.concatenate([bd(a), bd(b)], axis=-1).astype(BF16)


def _inproj_kernel(*refs, fuse_prev, alpha, n_part):
    if fuse_prev:
        (x_ref, y_ref, gf_ref, lg_ref, lb_ref, sh_ref, sc_ref, w_ref, wvt_ref, ab_ref,
         cq_ref, sq_ref, ck_ref, sk_ref, u_ref, q_ref, k_ref, v_ref, xo_ref) = refs
    else:
        (x_ref, sh_ref, sc_ref, w_ref, wvt_ref, ab_ref,
         cq_ref, sq_ref, ck_ref, sk_ref, u_ref, q_ref, k_ref, v_ref) = refs
    fw = FOURIER_WIDTH
    aw = (w_ref.shape[1] - fw) // 2
    nh = aw // LANES
    tp = x_ref.shape[1] // n_part
    lane = lax.broadcasted_iota(jnp.int32, (tp, LANES), 1)
    half = ROPE_AXIS_DIM // 2
    first_half = (lane & half) == 0

    for part in range(n_part):
        rows = slice(part * tp, (part + 1) * tp)
        if fuse_prev:
            y = _unpack_halves(_load_token_tiles(y_ref, part * tp, tp, x_ref.shape[2] // 2 // LANES))
            z = alpha * x_ref[0, rows] + gf_ref[0] * y
            x = _ln(z) * lg_ref[...] + lb_ref[...]
            xo_ref[0, rows] = x
        else:
            x = x_ref[0, rows]
        h = (_ln(x) * (1.0 + sc_ref[0]) + sh_ref[0]).astype(BF16)
        u = _dot(h, w_ref[:, :fw]).astype(BF16)
        u_ref[0, rows] = _dot(u, ab_ref[...]).astype(BF16)

        def rope_store(p, cos, sin, dst_ref):
            for j in range(nh):
                pj = p[:, j * LANES:(j + 1) * LANES]
                swapped = jnp.where(first_half, pltpu.roll(pj, LANES - half, 1), pltpu.roll(pj, half, 1))
                dst_ref[0, j, rows] = (pj * cos + swapped * sin).astype(BF16)

        rope_store(_dot(h, w_ref[:, fw:fw + aw]), cq_ref[rows], sq_ref[rows], q_ref)
        rope_store(_dot(h, w_ref[:, fw + aw:]), ck_ref[rows], sk_ref[rows], k_ref)
        vt = _dot_nt(wvt_ref[...], h)
        for j in range(nh):
            v_ref[0, j, :, rows] = vt[j * V_DIM:(j + 1) * V_DIM, :].astype(BF16)


def _inproj_call(x, shift, scale, w_uqk, w_vt, ab, tabs, prev=None, alpha=1.0, tm=512):
    bsz, t, d = x.shape
    aw = w_vt.shape[0]
    nh = aw // V_DIM
    tm = min(tm, t)
    grid = (bsz, t // tm)
    row = pl.BlockSpec((1, tm, d), lambda b, i: (b, i, 0))
    vec = pl.BlockSpec((1, 1, d), lambda b, i: (b, 0, 0))
    cst = lambda shp: pl.BlockSpec(shp, lambda b, i: (0,) * len(shp))
    tab = pl.BlockSpec((tm, LANES), lambda b, i: (i, 0))
    in_specs, args = [row], [x]
    if prev is not None:
        y, gf, lg, lb = prev
        nt = t // tm
        in_specs += [pl.BlockSpec((tm * ROW_TILE, LANES), lambda b, i: (b * nt + i, 0)), vec, cst((1, d)),
                     cst((1, d))]
        args += [y, gf, lg.reshape(1, d), lb.reshape(1, d)]
    in_specs += [vec, vec, cst(w_uqk.shape), cst(w_vt.shape), cst(ab.shape), tab, tab, tab, tab]
    args += [shift, scale, w_uqk, w_vt, ab, *tabs]
    head_rows = pl.BlockSpec((1, nh, tm, V_DIM), lambda b, i: (b, 0, i, 0))
    out_shape = [
        jax.ShapeDtypeStruct((bsz, t, 2 * FOURIER_WIDTH), BF16),
        jax.ShapeDtypeStruct((bsz, nh, t, V_DIM), BF16),
        jax.ShapeDtypeStruct((bsz, nh, t, V_DIM), BF16),
        jax.ShapeDtypeStruct((bsz, nh, V_DIM, t), BF16),
    ]
    out_specs = [
        pl.BlockSpec((1, tm, 2 * FOURIER_WIDTH), lambda b, i: (b, i, 0)),
        head_rows,
        head_rows,
        pl.BlockSpec((1, nh, V_DIM, tm), lambda b, i: (b, 0, 0, i)),
    ]
    if prev is not None:
        out_shape.append(jax.ShapeDtypeStruct((bsz, t, d), F32))
        out_specs.append(row)
    return pl.pallas_call(
        functools.partial(_inproj_kernel, fuse_prev=prev is not None, alpha=alpha,
                          n_part=2 if tm % (2 * LANES) == 0 else 1),
        out_shape=tuple(out_shape),
        grid=grid,
        in_specs=in_specs,
        out_specs=tuple(out_specs),
        compiler_params=_cparams(("arbitrary", "arbitrary")),
        name="inproj",
    )(*args)


def _attn_kernel(lam_ref, q_ref, *refs, n_seg, norm_scale, tq):
    kv_refs = refs[:2 * n_seg]
    g_ref, o_ref, s_buf, m_buf = refs[2 * n_seg:]
    nh, t = q_ref.shape[1], q_ref.shape[2]
    n_iter = (t // tq) * nh
    lane = lax.broadcasted_iota(jnp.int32, (tq, LANES), 1)
    lam = lam_ref[0]
    gain = g_ref[...] * norm_scale
    seg_len = [kv_refs[2 * j].shape[2] for j in range(n_seg)]
    seg_off = [sum(seg_len[:j]) for j in range(n_seg)]

    def where(it):
        h = lax.rem(it, nh)
        rows = pl.ds(pl.multiple_of(lax.div(it, nh) * tq, tq), tq)
        return h, rows

    def scores(it, sub):
        h, rows = where(it)
        qb = q_ref[0, h, rows, :]
        qs = jnp.where((lane < HEAD_DIM) == (sub == 0), qb, jnp.zeros_like(qb))
        m = None
        for j in range(n_seg):
            s = _dot_nt(kv_refs[2 * j][0, h], qs)
            s_buf[sub, seg_off[j]:seg_off[j] + seg_len[j], :] = s
            mj = jnp.max(s, axis=0, keepdims=True)
            m = mj if m is None else jnp.maximum(m, mj)
        m_buf[sub] = jnp.broadcast_to(m, m_buf.shape[1:])

    def attend(it, sub):
        h, _ = where(it)
        m = m_buf[sub][0:1]
        acc = None
        for j in range(n_seg):
            p = jnp.exp2(s_buf[sub, seg_off[j]:seg_off[j] + seg_len[j], :] - m).astype(BF16)
            vt = kv_refs[2 * j + 1][0, h]
            vext = jnp.concatenate([vt, jnp.ones((SUBLANES_BF16, vt.shape[1]), BF16)], axis=0)
            aj = _dot(vext, p)
            acc = aj if acc is None else acc + aj
        return acc[:V_DIM] / acc[V_DIM:V_DIM + 1]

    def finish(it, o0, o1):
        h, rows = where(it)
        o = o0 - lam * o1
        o = o * lax.rsqrt(jnp.mean(o * o, axis=0, keepdims=True) + LN_EPS)
        o_ref[0, h, rows, :] = (o.T * gain).astype(BF16)

    scores(0, 0)

    def step(it, carry):
        scores(it, 1)
        o0 = attend(it, 0)
        scores(it + 1, 0)
        o1 = attend(it, 1)
        finish(it, o0, o1)
        return carry

    lax.fori_loop(0, n_iter - 1, step, 0)
    last = n_iter - 1
    scores(last, 1)
    o0 = attend(last, 0)
    o1 = attend(last, 1)
    finish(last, o0, o1)


def _attn_call(q, kv_segs, lam, g, lam_init, tq=512):
    bsz, nh, t, _ = q.shape
    tq = min(tq, t)
    whole = lambda a: pl.BlockSpec((1,) + a.shape[1:], lambda b: (b, 0, 0, 0))
    in_specs = [pl.BlockSpec(memory_space=pltpu.SMEM), whole(q)]
    args = [lam.reshape(1).astype(F32), q]
    for k, vt in kv_segs:
        in_specs += [whole(k), whole(vt)]
        args += [k, vt]
    in_specs.append(pl.BlockSpec((1, V_DIM), lambda b: (0, 0)))
    args.append(g.reshape(1, V_DIM).astype(F32))
    n_keys = sum(k.shape[2] for k, _ in kv_segs)
    return pl.pallas_call(
        functools.partial(_attn_kernel, n_seg=len(kv_segs), norm_scale=1.0 - lam_init, tq=tq),
        out_shape=jax.ShapeDtypeStruct(q.shape, BF16),
        grid=(bsz,),
        in_specs=in_specs,
        out_specs=whole(q),
        scratch_shapes=[pltpu.VMEM((2, n_keys, tq), F32), pltpu.VMEM((2, 8, tq), F32)],
        compiler_params=_cparams(("arbitrary",)),
        name="diff_attn",
    )(*args)


def _fourier_kernel(c_ref, s_ref, u1_ref, u2_ref, o_ref):
    o_ref[0] = (_dot(c_ref[...], u1_ref[0]) - _dot(s_ref[...], u2_ref[0])).astype(BF16)


def _dft_mats(t, split=64):
    tt = jnp.arange(t, dtype=jnp.int32)[None, :]
    k1 = jnp.arange(t // split, dtype=jnp.int32)[:, None] * split
    k0 = jnp.arange(split, dtype=jnp.int32)[:, None]
    w = 2.0 * math.pi / t
    a = ((k1 * tt) % t).astype(F32) * w
    b = ((k0 * tt) % t).astype(F32) * w
    ca, sa = jnp.cos(a)[:, None, :], jnp.sin(a)[:, None, :]
    cb, sb = jnp.cos(b)[None, :, :], jnp.sin(b)[None, :, :]
    scale = 1.0 / math.sqrt(t * FOURIER_GROUP_DIM)
    cmat = ((ca * cb - sa * sb) * scale).reshape(t, t)
    smat = ((sa * cb + ca * sb) * scale).reshape(t, t)
    return cmat.astype(BF16), smat.astype(BF16)


def _fourier_call(u12, cmat, smat, tk=1024):
    bsz, t, w2 = u12.shape
    w = w2 // 2
    tk = min(tk, t)
    return pl.pallas_call(
        _fourier_kernel,
        out_shape=jax.ShapeDtypeStruct((bsz, t, w), BF16),
        grid=(t // tk, bsz),
        in_specs=[
            pl.BlockSpec((tk, t), lambda i, b: (i, 0)),
            pl.BlockSpec((tk, t), lambda i, b: (i, 0)),
            pl.BlockSpec((1, t, w), lambda i, b: (b, 0, 0)),
            pl.BlockSpec((1, t, w), lambda i, b: (b, 0, 1)),
        ],
        out_specs=pl.BlockSpec((1, tk, w), lambda i, b: (b, i, 0)),
        compiler_params=_cparams(("arbitrary", "arbitrary")),
        name="fourier_dft",
    )(cmat, smat, u12, u12)


def _pack_halves(v):
    n = v.shape[1] // 2
    bits = lax.bitcast_convert_type(v, jnp.uint32)
    return bits[:, :n] | (bits[:, n:] >> 16)


def _unpack_halves(u):
    hi = lax.bitcast_convert_type(u & jnp.uint32(0xFFFF0000), F32)
    lo = lax.bitcast_convert_type(u << 16, F32)
    return jnp.concatenate([hi, lo], axis=1)


ROW_TILE = 8


def _store_token_tiles(ref, first_tok, words):
    n = words.shape[0]
    k = words.shape[1] // LANES
    for c in range(ROW_TILE):
        chunk = words[:, c * LANES:(c + 1) * LANES] if c < k else jnp.zeros((n, LANES), jnp.uint32)
        ref[pl.ds(first_tok * ROW_TILE + c, n, stride=ROW_TILE), :] = chunk


def _load_token_tiles(ref, first_tok, n, k):
    return jnp.concatenate([ref[pl.ds(first_tok * ROW_TILE + c, n, stride=ROW_TILE), :] for c in range(k)], axis=1)


def _route_rows(logits, bias):
    score = jax.nn.sigmoid(logits)
    sel = score + bias
    r = [sel[j:j + 1, :] for j in range(N_EXPERTS)]
    s = [score[j:j + 1, :] for j in range(N_EXPERTS)]
    npg = EXPERTS_PER_GROUP
    best = None
    for g in range(N_GROUPS):
        v = r[g * npg:(g + 1) * npg]
        pair_sums = [v[a] + v[b] for a in range(npg) for b in range(a + 1, npg)]
        tg = functools.reduce(jnp.maximum, pair_sums)
        if best is None:
            best, bg = tg, jnp.zeros_like(tg)
        else:
            upd = tg > best
            best = jnp.where(upd, tg, best)
            bg = jnp.where(upd, float(g), bg)

    def of_group(rows, j):
        out = rows[j]
        for g in range(1, N_GROUPS):
            out = jnp.where(bg == float(g), rows[g * npg + j], out)
        return out

    v = [of_group(r, j) for j in range(npg)]
    sv = [of_group(s, j) for j in range(npg)]

    def first_argmax(vals):
        m = functools.reduce(jnp.maximum, vals)
        idx = jnp.full_like(m, float(npg - 1))
        for j in range(npg - 2, -1, -1):
            idx = jnp.where(vals[j] == m, float(j), idx)
        return idx

    i1 = first_argmax(v)
    i2 = first_argmax([jnp.where(i1 == float(j), -jnp.inf, v[j]) for j in range(npg)])
    lo = jnp.minimum(i1, i2)
    hi = jnp.maximum(i1, i2)

    def pick(vals, idx):
        out = vals[0]
        for j in range(1, npg):
            out = jnp.where(idx == float(j), vals[j], out)
        return out

    s_lo, s_hi = pick(sv, lo), pick(sv, hi)
    den = s_lo + s_hi
    pair = lo * (2.0 * npg - 1.0 - lo) * 0.5 + (hi - lo - 1.0)
    cls = bg * float(N_PAIR_CLASSES // N_GROUPS) + pair
    return cls, s_lo / den, s_hi / den


def _outproj_kernel(f_ref, o_ref, x_ref, ga_ref, lg_ref, lb_ref, sh_ref, sc_ref, w_ref, wr_ref, rb_ref, tri_ref,
                    cin_ref, xo_ref, hx_ref, rt_ref, cnt_ref, *, alpha, n_part):
    first = (pl.program_id(0) == 0) & (pl.program_id(1) == 0)

    @pl.when(first)
    def _():
        cnt_ref[...] = cin_ref[...]

    fw = f_ref.shape[2]
    tm = x_ref.shape[1]
    tp = tm // n_part
    logit_parts = []
    packed_parts = []
    for part in range(n_part):
        rows = slice(part * tp, (part + 1) * tp)
        o = jnp.concatenate([o_ref[0, j, rows] for j in range(o_ref.shape[1])], axis=1)
        proj = _dot(f_ref[0, rows], w_ref[:fw, :]) + _dot(o, w_ref[fw:, :])
        z = alpha * x_ref[0, rows] + ga_ref[0] * proj
        x = _ln(z) * lg_ref[...] + lb_ref[...]
        xo_ref[0, rows] = x
        h = _ln(x) * (1.0 + sc_ref[0]) + sh_ref[0]
        hh, hl = _split_bf16(h)
        packed_parts.append(_pack_halves(hh.astype(F32)))
        r = _dot_nt(wr_ref[...], hh)
        r2 = _dot_nt(wr_ref[:N_EXPERTS, :], hl)
        logit_parts.append(r[:N_EXPERTS] + r[N_EXPERTS:] + r2)
    logits = jnp.concatenate(logit_parts, axis=1)
    cls, w0, w1 = _route_rows(logits, rb_ref[...])

    ncls = cnt_ref.shape[0]
    cls_iota = lax.broadcasted_iota(jnp.int32, (ncls, tm), 0).astype(F32)
    onehot = cls_iota == cls
    oh = onehot.astype(F32)
    prefix = _dot(oh.astype(BF16), tri_ref[...])
    counts = cnt_ref[...]
    base = jnp.concatenate([counts] * (tm // LANES), axis=1)
    rank = jnp.sum(jnp.where(onehot, prefix + base, 0.0), axis=0, keepdims=True)
    cnt_ref[...] = counts + jnp.sum(oh, axis=1, keepdims=True)

    row_iota = lax.broadcasted_iota(jnp.int32, (rt_ref.shape[0], tm), 0)
    rt_ref[...] = jnp.where(row_iota == 0, cls, jnp.where(row_iota == 1, rank, 0.0))

    wrow_iota = lax.broadcasted_iota(jnp.int32, (LANES, tm), 0)
    wcols = jnp.where(wrow_iota == 0, w0, jnp.where(wrow_iota == 1, w1, 0.0)).T
    packed = packed_parts[0] if n_part == 1 else jnp.concatenate(packed_parts, axis=0)
    _store_token_tiles(hx_ref, 0, jnp.concatenate([packed, lax.bitcast_convert_type(wcols, jnp.uint32)], axis=1))


def _outproj_call(f, o, x, g_a, ln_g, ln_b, shift, scale, w_out, wr_hilo, router_bias, counts_in, alpha, tm=512):
    bsz, t, d = x.shape
    tm = min(tm, t)
    nt = t // tm
    assert d // 2 + LANES <= ROW_TILE * LANES
    row = lambda w: pl.BlockSpec((1, tm, w), lambda b, i: (b, i, 0))
    vec = pl.BlockSpec((1, 1, d), lambda b, i: (b, 0, 0))
    cst = lambda shp: pl.BlockSpec(shp, lambda b, i: (0,) * len(shp))
    tri = jnp.asarray(np.triu(np.ones((tm, tm), np.float32), 1), BF16)
    return pl.pallas_call(
        functools.partial(_outproj_kernel, alpha=alpha, n_part=1),
        out_shape=(
            jax.ShapeDtypeStruct((bsz, t, d), F32),
            jax.ShapeDtypeStruct((bsz * t * ROW_TILE, LANES), jnp.uint32),
            jax.ShapeDtypeStruct((8, bsz * t), F32),
            jax.ShapeDtypeStruct(counts_in.shape, F32),
        ),
        grid=(bsz, nt),
        in_specs=[row(f.shape[2]), pl.BlockSpec((1, o.shape[1], tm, V_DIM), lambda b, i: (b, 0, i, 0)), row(d),
                  vec, cst((1, d)), cst((1, d)), vec, vec, cst(w_out.shape), cst(wr_hilo.shape),
                  cst((N_EXPERTS, 1)), cst((tm, tm)), cst(counts_in.shape)],
        out_specs=(row(d), pl.BlockSpec((tm * ROW_TILE, LANES), lambda b, i: (b * nt + i, 0)),
                   pl.BlockSpec((8, tm), lambda b, i: (0, b * nt + i)), cst(counts_in.shape)),
        compiler_params=_cparams(("arbitrary", "arbitrary")),
        name="outproj",
    )(f, o, x, g_a, ln_g.reshape(1, d), ln_b.reshape(1, d), shift, scale, w_out, wr_hilo,
      router_bias.reshape(N_EXPERTS, 1).astype(F32), tri, counts_in)


COPY_UNROLL = 8


def _token_copies(n_tok, src_tok, dst_tok, sem):
    def start(g, carry):
        for j in range(COPY_UNROLL):
            r = g * COPY_UNROLL + j
            pltpu.make_async_copy(src_tok(r), dst_tok(r), sem).start(priority=j % 2)
        return carry

    def wait(g, carry):
        for j in range(COPY_UNROLL):
            pltpu.make_async_copy(src_tok(0), dst_tok(0), sem).wait()
        return carry

    lax.fori_loop(0, n_tok // COPY_UNROLL, start, 0)
    lax.fori_loop(0, n_tok // COPY_UNROLL, wait, 0)


def _scatter_tokens_kernel(pos_ref, src_ref, dst_in_ref, dst_ref, sem):
    del dst_in_ref
    _token_copies(src_ref.shape[0], lambda r: src_ref.at[r], lambda r: dst_ref.at[pos_ref[0, 0, r]], sem)


def _scatter_tokens_call(src, pos, dst, toks_per_step=512):
    t = pos.shape[0]
    ts = min(toks_per_step, t)
    tile = (ROW_TILE, LANES)
    out = pl.pallas_call(
        _scatter_tokens_kernel,
        out_shape=jax.ShapeDtypeStruct((dst.shape[0] // ROW_TILE,) + tile, dst.dtype),
        grid=(t // ts,),
        in_specs=[pl.BlockSpec((1, 1, ts), lambda i: (i, 0, 0), memory_space=pltpu.SMEM),
                  pl.BlockSpec((ts,) + tile, lambda i: (i, 0, 0)),
                  pl.BlockSpec(memory_space=pl.ANY)],
        out_specs=pl.BlockSpec(memory_space=pl.ANY),
        scratch_shapes=[pltpu.SemaphoreType.DMA(())],
        input_output_aliases={2: 0},
        compiler_params=_cparams(("arbitrary",)),
        name="scatter_tokens",
    )(pos.reshape(t // ts, 1, ts), src.reshape((t,) + tile), dst.reshape((-1,) + tile))
    return out.reshape(dst.shape)


def _gather_tokens_kernel(pos_ref, src_ref, dst_ref, sem):
    _token_copies(dst_ref.shape[0], lambda r: src_ref.at[pos_ref[0, 0, r]], lambda r: dst_ref.at[r], sem)


def _gather_tokens_call(src, pos, toks_per_step=512):
    t = pos.shape[0]
    ts = min(toks_per_step, t)
    tile = (ROW_TILE, LANES)
    out = pl.pallas_call(
        _gather_tokens_kernel,
        out_shape=jax.ShapeDtypeStruct((t,) + tile, src.dtype),
        grid=(t // ts,),
        in_specs=[pl.BlockSpec((1, 1, ts), lambda i: (i, 0, 0), memory_space=pltpu.SMEM),
                  pl.BlockSpec(memory_space=pl.ANY)],
        out_specs=pl.BlockSpec((ts,) + tile, lambda i: (i, 0, 0)),
        scratch_shapes=[pltpu.SemaphoreType.DMA(())],
        compiler_params=_cparams(("arbitrary",)),
        name="gather_tokens",
    )(pos.reshape(t // ts, 1, ts), src.reshape((-1,) + tile))
    return out.reshape(t * ROW_TILE, LANES)


def _moe_kernel(e0_ref, e1_ref, valid_ref, fresh_ref, x_ref, wg0, wu0, wd0, wg1, wu1, wd1, y_ref,
                wg_s, wu_s, wd_s):
    i = pl.program_id(0)
    half = wg_s.shape[1] // 2

    @pl.when(fresh_ref[i] != 0)
    def _():
        for slot, (wg, wu, wd) in enumerate(((wg0, wu0, wd0), (wg1, wu1, wd1))):
            wg_s[slot] = wg[0, 0].astype(BF16)
            wu_s[slot] = wu[0, 0].astype(BF16)
            wd_s[slot] = wd[0, 0].astype(BF16)

    @pl.when(valid_ref[i] != 0)
    def _():
        n_tok = x_ref.shape[0] // ROW_TILE
        xrow = _load_token_tiles(x_ref, 0, n_tok, half // LANES + 1)
        x = _unpack_halves(xrow[:, :half]).astype(BF16)
        wt = lax.bitcast_convert_type(xrow[:, half:], F32)

        def expert(slot):
            g = _dot(x, wg_s[slot])
            u = _dot(x, wu_s[slot])
            a = (g * jax.nn.sigmoid(g) * u).astype(BF16)
            return _dot(a, wd_s[slot])

        y = wt[:, 0:1] * expert(0) + wt[:, 1:2] * expert(1)
        _store_token_tiles(y_ref, 0, _pack_halves(y.astype(BF16).astype(F32)))

    @pl.when(valid_ref[i] == 0)
    def _():
        y_ref[...] = jnp.zeros_like(y_ref)


def _moe_call(xs, tile_e0, tile_e1, tile_valid, tile_fresh, w_gate, w_up, w_down, layer, tmm):
    tpad = xs.shape[0] // ROW_TILE
    _, _, d, de = w_gate.shape
    n_tiles = tpad // tmm
    tok_tiles = pl.BlockSpec((tmm * ROW_TILE, LANES), lambda i, e0, e1, vl, fr: (i, 0))
    wspec0 = lambda shp: pl.BlockSpec((1, 1) + shp, lambda i, e0, e1, vl, fr: (layer, e0[i], 0, 0))
    wspec1 = lambda shp: pl.BlockSpec((1, 1) + shp, lambda i, e0, e1, vl, fr: (layer, e1[i], 0, 0))
    grid_spec = pltpu.PrefetchScalarGridSpec(
        num_scalar_prefetch=4,
        grid=(n_tiles,),
        in_specs=[
            tok_tiles,
            wspec0((d, de)), wspec0((d, de)), wspec0((de, d)),
            wspec1((d, de)), wspec1((d, de)), wspec1((de, d)),
        ],
        out_specs=tok_tiles,
        scratch_shapes=[pltpu.VMEM((2, d, de), BF16), pltpu.VMEM((2, d, de), BF16), pltpu.VMEM((2, de, d), BF16)],
    )
    return pl.pallas_call(
        _moe_kernel,
        out_shape=jax.ShapeDtypeStruct(xs.shape, jnp.uint32),
        grid_spec=grid_spec,
        compiler_params=_cparams(("arbitrary",)),
        name="moe_ffn",
    )(tile_e0, tile_e1, tile_valid, tile_fresh, xs, w_gate, w_up, w_down, w_gate, w_up, w_down)


def _pair_tables():
    e0s, e1s = [], []
    for g in range(N_GROUPS):
        for a in range(EXPERTS_PER_GROUP):
            for b in range(a + 1, EXPERTS_PER_GROUP):
                e0s.append(g * EXPERTS_PER_GROUP + a)
                e1s.append(g * EXPERTS_PER_GROUP + b)
    return np.asarray(e0s, np.int32), np.asarray(e1s, np.int32)


def _moe_ffn(rows, routes, counts, w_gate, w_up, w_down, layer, tmm=512):
    t_all = sum(rt.shape[1] for rt in routes)
    cnt = counts[:N_PAIR_CLASSES, 0].astype(jnp.int32)
    padded = ((cnt + tmm - 1) // tmm) * tmm
    ends = jnp.cumsum(padded)
    starts = ends - padded
    n_tiles = -(-t_all // tmm) + N_PAIR_CLASSES
    tpad = n_tiles * tmm
    total = ends[-1]
    tile_start = jnp.arange(n_tiles, dtype=jnp.int32) * tmm
    tile_valid = (tile_start < total).astype(jnp.int32)
    probe = jnp.minimum(tile_start, total - 1)
    tile_cls = jnp.sum((ends[None, :] <= probe[:, None]).astype(jnp.int32), axis=1)
    tile_cls = jnp.minimum(tile_cls, N_PAIR_CLASSES - 1)
    pe0, pe1 = _pair_tables()
    tile_e0 = jnp.asarray(pe0)[tile_cls]
    tile_e1 = jnp.asarray(pe1)[tile_cls]
    tile_fresh = jnp.concatenate([jnp.ones((1,), jnp.int32),
                                  (tile_cls[1:] != tile_cls[:-1]).astype(jnp.int32)])
    cls_ids = jnp.arange(N_PAIR_CLASSES, dtype=F32)[:, None]
    starts_f = starts.astype(F32)[:, None]
    poss = []
    for rt in routes:
        start_of_tok = jnp.sum(jnp.where(rt[0][None, :] == cls_ids, starts_f, 0.0), axis=0)
        poss.append((start_of_tok + rt[1]).astype(jnp.int32))
    xs = jnp.zeros((tpad * ROW_TILE, LANES), jnp.uint32)
    for r, pos in zip(rows, poss):
        xs = _scatter_tokens_call(r, pos, xs)
    ys = _moe_call(xs, tile_e0, tile_e1, tile_valid, tile_fresh, w_gate, w_up, w_down, layer, tmm)
    return [_gather_tokens_call(ys, pos) for pos in poss]


def _ffn_ln_kernel(x_ref, y_ref, gf_ref, lg_ref, lb_ref, o_ref, *, alpha):
    tm, d = x_ref.shape[1], x_ref.shape[2]
    y = _unpack_halves(_load_token_tiles(y_ref, 0, tm, d // 2 // LANES))
    z = alpha * x_ref[0] + gf_ref[0] * y
    o_ref[0] = _ln(z) * lg_ref[...] + lb_ref[...]


def _ffn_ln_call(x, y, g_f, ln_g, ln_b, alpha, tm=512):
    bsz, t, d = x.shape
    tm = min(tm, t)
    row = pl.BlockSpec((1, tm, d), lambda b, i: (b, i, 0))
    vec = pl.BlockSpec((1, 1, d), lambda b, i: (b, 0, 0))
    cst = pl.BlockSpec((1, d), lambda b, i: (0, 0))
    return pl.pallas_call(
        functools.partial(_ffn_ln_kernel, alpha=alpha),
        out_shape=jax.ShapeDtypeStruct((bsz, t, d), F32),
        grid=(bsz, t // tm),
        in_specs=[row, pl.BlockSpec((tm * ROW_TILE, LANES), lambda b, i: (b * (t // tm) + i, 0)), vec, cst, cst],
        out_specs=row,
        compiler_params=_cparams(("arbitrary", "arbitrary")),
        name="ffn_ln",
    )(x, y, g_f, ln_g.reshape(1, d), ln_b.reshape(1, d))


def _rope_tables(s):
    rows = s // GRID_W
    row = jnp.repeat(jnp.arange(rows), GRID_W).astype(F32)
    col = jnp.tile(jnp.arange(GRID_W), rows).astype(F32)
    freqs = ROPE_BASE ** (-jnp.arange(0, ROPE_AXIS_DIM, 2, dtype=F32) / ROPE_AXIS_DIM)
    ang_row = row[:, None] * freqs
    ang_col = col[:, None] * freqs

    def head(r, c, sign):
        return jnp.concatenate([sign * r, r, sign * c, c], axis=1)

    cos = head(jnp.cos(ang_row), jnp.cos(ang_col), 1.0)
    sin = head(jnp.sin(ang_row), jnp.sin(ang_col), -1.0)
    cos = jnp.concatenate([cos, cos], axis=1)
    sin = jnp.concatenate([sin, sin], axis=1)
    return cos, sin


def kernel(x, c, ctx, c_ctx, w_mod, b_mod, w_in, w_fourier, lam_qk, subln_g, w_out, ln_attn_g, ln_attn_b,
           ln_ffn_g, ln_ffn_b, w_router, router_bias, w_gate, w_up, w_down):
    bsz, s, d = x.shape
    l_ctx = ctx.shape[1]
    depth = w_mod.shape[0]
    alpha = (2 * depth) ** 0.25
    qscale = LOG2E * HEAD_DIM ** -0.5

    pad = (-(bsz + 1)) % 8
    cc = jnp.concatenate([c, c_ctx[None, :], jnp.zeros((pad, d), F32)], axis=0)
    mod = _mod_call(cc, w_mod, b_mod)

    ab = _fprep_call(w_fourier)
    v_off = w_in.shape[2] - (w_in.shape[2] - FOURIER_WIDTH) // 3
    w_uqk = w_in[:, :, :v_off].astype(BF16)
    w_vt = jnp.swapaxes(w_in[:, :, v_off:], 1, 2).astype(BF16)
    w_out_b = w_out.astype(BF16)
    wr_t = w_router.T.astype(F32)
    wr_hi = wr_t.astype(BF16)
    wr_lo = (wr_t - wr_hi.astype(F32)).astype(BF16)
    wr_hilo = jnp.concatenate([wr_hi, wr_lo], axis=0)

    cos, sin = _rope_tables(s)
    tabs_lat = (cos * qscale, sin * qscale, cos, sin)
    ones = jnp.ones((l_ctx, LANES), F32)
    zeros = jnp.zeros((l_ctx, LANES), F32)
    tabs_ctx = (ones * qscale, zeros, ones, zeros)
    dft_lat = _dft_mats(s)
    dft_ctx = _dft_mats(l_ctx)

    xc = ctx
    prev = None
    prev_c = None
    for l in range(depth):
        last = l == depth - 1
        lam_init = 0.8 - 0.6 * math.exp(-0.3 * l)
        lq = lam_qk[l].astype(F32)
        lam = jnp.exp(jnp.sum(lq[0] * lq[1])) - jnp.exp(jnp.sum(lq[2] * lq[3])) + lam_init

        m_lat = mod[l, :bsz].reshape(bsz, 1, 6, d)
        sh_a, sc_a, g_a, sh_f, sc_f, g_f = [m_lat[:, :, i, :] for i in range(6)]
        m_ctx = jnp.broadcast_to(mod[l, bsz].reshape(1, 1, 6, d), (bsz, 1, 6, d))
        csh_a, csc_a, cg_a, csh_f, csc_f, cg_f = [m_ctx[:, :, i, :] for i in range(6)]

        outs = _inproj_call(x, sh_a, sc_a, w_uqk[l], w_vt[l], ab[l], tabs_lat, prev=prev, alpha=alpha)
        u12, q, k, v = outs[:4]
        if prev is not None:
            x = outs[4]
        outs_c = _inproj_call(xc, csh_a, csc_a, w_uqk[l], w_vt[l], ab[l], tabs_ctx, prev=prev_c, alpha=alpha)
        u12c, qc, kc, vc = outs_c[:4]
        if prev_c is not None:
            xc = outs_c[4]

        o_attn = _attn_call(q, [(kc, vc), (k, v)], lam, subln_g[l], lam_init)
        f = _fourier_call(u12, *dft_lat)
        counts = jnp.zeros((32, LANES), F32)
        rows, routes = [], []
        if not last:
            oc_attn = _attn_call(qc, [(kc, vc)], lam, subln_g[l], lam_init)
            fc = _fourier_call(u12c, *dft_ctx)
            xc, hxc, route_c, counts = _outproj_call(fc, oc_attn, xc, cg_a, ln_attn_g[l], ln_attn_b[l], csh_f,
                                                     csc_f, w_out_b[l], wr_hilo, router_bias, counts, alpha)
            rows.append(hxc)
            routes.append(route_c)
        x, hx, route_t, counts = _outproj_call(f, o_attn, x, g_a, ln_attn_g[l], ln_attn_b[l], sh_f, sc_f,
                                               w_out_b[l], wr_hilo, router_bias, counts, alpha)
        rows.append(hx)
        routes.append(route_t)

        ys = _moe_ffn(rows, routes, counts, w_gate, w_up, w_down, l)
        if not last:
            prev_c = (ys[0], cg_f, ln_ffn_g[l], ln_ffn_b[l])
        prev = (ys[-1], g_f, ln_ffn_g[l], ln_ffn_b[l])

    y, g_f, lg, lb = prev
    return _ffn_ln_call(x, y, g_f, lg, lb, alpha)
```

```python
import functools
import math

import jax
import jax.numpy as jnp
import numpy as np
from jax import lax
from jax.experimental import pallas as pl
from jax.experimental.pallas import tpu as pltpu

F32 = jnp.float32
BF16 = jnp.bfloat16

GRID_W = 64
FOURIER_WIDTH = 256
FOURIER_GROUPS = 4
FOURIER_GROUP_DIM = FOURIER_WIDTH // FOURIER_GROUPS
HEAD_DIM = 64
V_DIM = 2 * HEAD_DIM
ROPE_BASE = 10000.0
ROPE_AXIS_DIM = HEAD_DIM // 2
N_EXPERTS = 16
N_GROUPS = 4
EXPERTS_PER_GROUP = N_EXPERTS // N_GROUPS
N_PAIR_CLASSES = N_GROUPS * (EXPERTS_PER_GROUP * (EXPERTS_PER_GROUP - 1) // 2)
LN_EPS = 1e-5
LOG2E = math.log2(math.e)

LANES = 128
SUBLANES_BF16 = 16
VMEM_LIMIT = 48 * 1024 * 1024


def _cparams(sem):
    return pltpu.CompilerParams(dimension_semantics=sem, vmem_limit_bytes=VMEM_LIMIT)


def _split_bf16(a):
    hi = a.astype(BF16)
    lo = (a - hi.astype(F32)).astype(BF16)
    return hi, lo


def _dot(a, b):
    return jnp.dot(a, b, preferred_element_type=F32)


def _dot_nt(a, b):
    return lax.dot_general(a, b, (((1,), (1,)), ((), ())), preferred_element_type=F32)


def _dot3(a, w):
    ah, al = _split_bf16(a)
    wh, wl = _split_bf16(w)
    return _dot(ah, wh) + _dot(ah, wl) + _dot(al, wh)


def _ln(x):
    mu = jnp.mean(x, axis=-1, keepdims=True)
    xc = x - mu
    var = jnp.mean(xc * xc, axis=-1, keepdims=True)
    return xc * lax.rsqrt(var + LN_EPS)


def _mod_kernel(c_ref, w_ref, b_ref, o_ref):
    c = c_ref[...]
    s = c * jax.nn.sigmoid(c)
    o_ref[0] = _dot3(s, w_ref[0]) + b_ref[0]


def _mod_call(cc, w_mod, b_mod):
    depth, d, n = w_mod.shape
    r = cc.shape[0]
    tn = 1536
    return pl.pallas_call(
        _mod_kernel,
        out_shape=jax.ShapeDtypeStruct((depth, r, n), F32),
        grid=(depth, n // tn),
        in_specs=[
            pl.BlockSpec((r, d), lambda l, j: (0, 0)),
            pl.BlockSpec((1, d, tn), lambda l, j: (l, 0, j)),
            pl.BlockSpec((1, 1, tn), lambda l, j: (l, 0, j)),
        ],
        out_specs=pl.BlockSpec((1, r, tn), lambda l, j: (l, 0, j)),
        compiler_params=_cparams(("arbitrary", "arbitrary")),
        name="mod",
    )(cc, w_mod, b_mod.reshape(depth, 1, n))


def _fprep_kernel(c_ref, s_ref, w_ref, a_ref, b_ref):
    w = w_ref[0]
    a_ref[0] = _dot3(c_ref[...], w)
    b_ref[0] = _dot3(s_ref[...], w)


def _fprep_call(w_fourier):
    depth, g, c, _ = w_fourier.shape
    idx = np.arange(c)
    ang = 2.0 * np.pi * ((idx[:, None] * idx[None, :]) % c) / c
    c64 = jnp.asarray(np.cos(ang), F32)
    s64 = jnp.asarray(np.sin(ang), F32)
    wf = w_fourier.reshape(depth * g, c, c)
    spec = pl.BlockSpec((1, c, c), lambda i: (i, 0, 0))
    cst = pl.BlockSpec((c, c), lambda i: (0, 0))
    a, b = pl.pallas_call(
        _fprep_kernel,
        out_shape=(jax.ShapeDtypeStruct(wf.shape, F32),) * 2,
        grid=(depth * g,),
        in_specs=[cst, cst, spec],
        out_specs=(spec, spec),
        compiler_params=_cparams(("arbitrary",)),
        name="fourier_prep",
    )(c64, s64, wf)
    eye = jnp.eye(g, dtype=F32)

    def bd(m):
        m = m.reshape(depth, g, c, c)
        return (m[:, :, :, None, :] * eye[None, :, None, :, None]).reshape(depth, g * c, g * c)

    return jnp.concatenate([bd(a), bd(b)], axis=-1).astype(BF16)


def _inproj_kernel(*refs, fuse_prev, alpha, n_part):
    if fuse_prev:
        (x_ref, y_ref, gf_ref, lg_ref, lb_ref, sh_ref, sc_ref, w_ref, wvt_ref, ab_ref,
         cq_ref, sq_ref, ck_ref, sk_ref, u_ref, q_ref, k_ref, v_ref, xo_ref) = refs
    else:
        (x_ref, sh_ref, sc_ref, w_ref, wvt_ref, ab_ref,
         cq_ref, sq_ref, ck_ref, sk_ref, u_ref, q_ref, k_ref, v_ref) = refs
    fw = FOURIER_WIDTH
    aw = (w_ref.shape[1] - fw) // 2
    nh = aw // LANES
    tp = x_ref.shape[1] // n_part
    lane = lax.broadcasted_iota(jnp.int32, (tp, LANES), 1)
    half = ROPE_AXIS_DIM // 2
    first_half = (lane & half) == 0

    for part in range(n_part):
        rows = slice(part * tp, (part + 1) * tp)
        if fuse_prev:
            y = _unpack_halves(_load_token_tiles(y_ref, part * tp, tp, x_ref.shape[2] // 2 // LANES))
            z = alpha * x_ref[0, rows] + gf_ref[0] * y
            x = _ln(z) * lg_ref[...] + lb_ref[...]
            xo_ref[0, rows] = x
        else:
            x = x_ref[0, rows]
        h = (_ln(x) * (1.0 + sc_ref[0]) + sh_ref[0]).astype(BF16)
        u = _dot(h, w_ref[:, :fw]).astype(BF16)
        u_ref[0, rows] = _dot(u, ab_ref[...]).astype(BF16)

        def rope_store(p, cos, sin, dst_ref):
            for j in range(nh):
                pj = p[:, j * LANES:(j + 1) * LANES]
                swapped = jnp.where(first_half, pltpu.roll(pj, LANES - half, 1), pltpu.roll(pj, half, 1))
                dst_ref[0, j, rows] = (pj * cos + swapped * sin).astype(BF16)

        rope_store(_dot(h, w_ref[:, fw:fw + aw]), cq_ref[rows], sq_ref[rows], q_ref)
        rope_store(_dot(h, w_ref[:, fw + aw:]), ck_ref[rows], sk_ref[rows], k_ref)
        vt = _dot_nt(wvt_ref[...], h)
        for j in range(nh):
            v_ref[0, j, :, rows] = vt[j * V_DIM:(j + 1) * V_DIM, :].astype(BF16)


def _inproj_call(x, shift, scale, w_uqk, w_vt, ab, tabs, prev=None, alpha=1.0, tm=512):
    bsz, t, d = x.shape
    aw = w_vt.shape[0]
    nh = aw // V_DIM
    tm = min(tm, t)
    grid = (bsz, t // tm)
    row = pl.BlockSpec((1, tm, d), lambda b, i: (b, i, 0))
    vec = pl.BlockSpec((1, 1, d), lambda b, i: (b, 0, 0))
    cst = lambda shp: pl.BlockSpec(shp, lambda b, i: (0,) * len(shp))
    tab = pl.BlockSpec((tm, LANES), lambda b, i: (i, 0))
    in_specs, args = [row], [x]
    if prev is not None:
        y, gf, lg, lb = prev
        nt = t // tm
        in_specs += [pl.BlockSpec((tm * ROW_TILE, LANES), lambda b, i: (b * nt + i, 0)), vec, cst((1, d)),
                     cst((1, d))]
        args += [y, gf, lg.reshape(1, d), lb.reshape(1, d)]
    in_specs += [vec, vec, cst(w_uqk.shape), cst(w_vt.shape), cst(ab.shape), tab, tab, tab, tab]
    args += [shift, scale, w_uqk, w_vt, ab, *tabs]
    head_rows = pl.BlockSpec((1, nh, tm, V_DIM), lambda b, i: (b, 0, i, 0))
    out_shape = [
        jax.ShapeDtypeStruct((bsz, t, 2 * FOURIER_WIDTH), BF16),
        jax.ShapeDtypeStruct((bsz, nh, t, V_DIM), BF16),
        jax.ShapeDtypeStruct((bsz, nh, t, V_DIM), BF16),
        jax.ShapeDtypeStruct((bsz, nh, V_DIM, t), BF16),
    ]
    out_specs = [
        pl.BlockSpec((1, tm, 2 * FOURIER_WIDTH), lambda b, i: (b, i, 0)),
        head_rows,
        head_rows,
        pl.BlockSpec((1, nh, V_DIM, tm), lambda b, i: (b, 0, 0, i)),
    ]
    if prev is not None:
        out_shape.append(jax.ShapeDtypeStruct((bsz, t, d), F32))
        out_specs.append(row)
    return pl.pallas_call(
        functools.partial(_inproj_kernel, fuse_prev=prev is not None, alpha=alpha,
                          n_part=2 if tm % (2 * LANES) == 0 else 1),
        out_shape=tuple(out_shape),
        grid=grid,
        in_specs=in_specs,
        out_specs=tuple(out_specs),
        compiler_params=_cparams(("arbitrary", "arbitrary")),
        name="inproj",
    )(*args)


def _attn_kernel(lam_ref, q_ref, *refs, n_seg, norm_scale, tq):
    kv_refs = refs[:2 * n_seg]
    g_ref, o_ref, s_buf, m_buf = refs[2 * n_seg:]
    nh, t = q_ref.shape[1], q_ref.shape[2]
    n_iter = (t // tq) * nh
    lane = lax.broadcasted_iota(jnp.int32, (tq, LANES), 1)
    lam = lam_ref[0]
    gain = g_ref[...] * norm_scale
    seg_len = [kv_refs[2 * j].shape[2] for j in range(n_seg)]
    seg_off = [sum(seg_len[:j]) for j in range(n_seg)]

    def where(it):
        h = lax.rem(it, nh)
        rows = pl.ds(pl.multiple_of(lax.div(it, nh) * tq, tq), tq)
        return h, rows

    def scores(it, sub):
        h, rows = where(it)
        qb = q_ref[0, h, rows, :]
        qs = jnp.where((lane < HEAD_DIM) == (sub == 0), qb, jnp.zeros_like(qb))
        m = None
        for j in range(n_seg):
            s = _dot_nt(kv_refs[2 * j][0, h], qs)
            s_buf[sub, seg_off[j]:seg_off[j] + seg_len[j], :] = s
            mj = jnp.max(s, axis=0, keepdims=True)
            m = mj if m is None else jnp.maximum(m, mj)
        m_buf[sub] = jnp.broadcast_to(m, m_buf.shape[1:])

    def attend(it, sub):
        h, _ = where(it)
        m = m_buf[sub][0:1]
        acc = None
        for j in range(n_seg):
            p = jnp.exp2(s_buf[sub, seg_off[j]:seg_off[j] + seg_len[j], :] - m).astype(BF16)
            vt = kv_refs[2 * j + 1][0, h]
            vext = jnp.concatenate([vt, jnp.ones((SUBLANES_BF16, vt.shape[1]), BF16)], axis=0)
            aj = _dot(vext, p)
            acc = aj if acc is None else acc + aj
        return acc[:V_DIM] / acc[V_DIM:V_DIM + 1]

    def finish(it, o0, o1):
        h, rows = where(it)
        o = o0 - lam * o1
        o = o * lax.rsqrt(jnp.mean(o * o, axis=0, keepdims=True) + LN_EPS)
        o_ref[0, h, rows, :] = (o.T * gain).astype(BF16)

    scores(0, 0)

    def step(it, carry):
        scores(it, 1)
        o0 = attend(it, 0)
        scores(it + 1, 0)
        o1 = attend(it, 1)
        finish(it, o0, o1)
        return carry

    lax.fori_loop(0, n_iter - 1, step, 0)
    last = n_iter - 1
    scores(last, 1)
    o0 = attend(last, 0)
    o1 = attend(last, 1)
    finish(last, o0, o1)


def _attn_call(q, kv_segs, lam, g, lam_init, tq=512):
    bsz, nh, t, _ = q.shape
    tq = min(tq, t)
    whole = lambda a: pl.BlockSpec((1,) + a.shape[1:], lambda b: (b, 0, 0, 0))
    in_specs = [pl.BlockSpec(memory_space=pltpu.SMEM), whole(q)]
    args = [lam.reshape(1).astype(F32), q]
    for k, vt in kv_segs:
        in_specs += [whole(k), whole(vt)]
        args += [k, vt]
    in_specs.append(pl.BlockSpec((1, V_DIM), lambda b: (0, 0)))
    args.append(g.reshape(1, V_DIM).astype(F32))
    n_keys = sum(k.shape[2] for k, _ in kv_segs)
    return pl.pallas_call(
        functools.partial(_attn_kernel, n_seg=len(kv_segs), norm_scale=1.0 - lam_init, tq=tq),
        out_shape=jax.ShapeDtypeStruct(q.shape, BF16),
        grid=(bsz,),
        in_specs=in_specs,
        out_specs=whole(q),
        scratch_shapes=[pltpu.VMEM((2, n_keys, tq), F32), pltpu.VMEM((2, 8, tq), F32)],
        compiler_params=_cparams(("arbitrary",)),
        name="diff_attn",
    )(*args)


def _fourier_kernel(c_ref, s_ref, u1_ref, u2_ref, o_ref):
    o_ref[0] = (_dot(c_ref[...], u1_ref[0]) - _dot(s_ref[...], u2_ref[0])).astype(BF16)


def _dft_mats(t, split=64):
    tt = jnp.arange(t, dtype=jnp.int32)[None, :]
    k1 = jnp.arange(t // split, dtype=jnp.int32)[:, None] * split
    k0 = jnp.arange(split, dtype=jnp.int32)[:, None]
    w = 2.0 * math.pi / t
    a = ((k1 * tt) % t).astype(F32) * w
    b = ((k0 * tt) % t).astype(F32) * w
    ca, sa = jnp.cos(a)[:, None, :], jnp.sin(a)[:, None, :]
    cb, sb = jnp.cos(b)[None, :, :], jnp.sin(b)[None, :, :]
    scale = 1.0 / math.sqrt(t * FOURIER_GROUP_DIM)
    cmat = ((ca * cb - sa * sb) * scale).reshape(t, t)
    smat = ((sa * cb + ca * sb) * scale).reshape(t, t)
    return cmat.astype(BF16), smat.astype(BF16)


def _fourier_call(u12, cmat, smat, tk=1024):
    bsz, t, w2 = u12.shape
    w = w2 // 2
    tk = min(tk, t)
    return pl.pallas_call(
        _fourier_kernel,
        out_shape=jax.ShapeDtypeStruct((bsz, t, w), BF16),
        grid=(t // tk, bsz),
        in_specs=[
            pl.BlockSpec((tk, t), lambda i, b: (i, 0)),
            pl.BlockSpec((tk, t), lambda i, b: (i, 0)),
            pl.BlockSpec((1, t, w), lambda i, b: (b, 0, 0)),
            pl.BlockSpec((1, t, w), lambda i, b: (b, 0, 1)),
        ],
        out_specs=pl.BlockSpec((1, tk, w), lambda i, b: (b, i, 0)),
        compiler_params=_cparams(("arbitrary", "arbitrary")),
        name="fourier_dft",
    )(cmat, smat, u12, u12)


def _pack_halves(v):
    n = v.shape[1] // 2
    bits = lax.bitcast_convert_type(v, jnp.uint32)
    return bits[:, :n] | (bits[:, n:] >> 16)


def _unpack_halves(u):
    hi = lax.bitcast_convert_type(u & jnp.uint32(0xFFFF0000), F32)
    lo = lax.bitcast_convert_type(u << 16, F32)
    return jnp.concatenate([hi, lo], axis=1)


ROW_TILE = 8


def _store_token_tiles(ref, first_tok, words):
    n = words.shape[0]
    k = words.shape[1] // LANES
    for c in range(ROW_TILE):
        chunk = words[:, c * LANES:(c + 1) * LANES] if c < k else jnp.zeros((n, LANES), jnp.uint32)
        ref[pl.ds(first_tok * ROW_TILE + c, n, stride=ROW_TILE), :] = chunk


def _load_token_tiles(ref, first_tok, n, k):
    return jnp.concatenate([ref[pl.ds(first_tok * ROW_TILE + c, n, stride=ROW_TILE), :] for c in range(k)], axis=1)


def _route_rows(logits, bias):
    score = jax.nn.sigmoid(logits)
    sel = score + bias
    r = [sel[j:j + 1, :] for j in range(N_EXPERTS)]
    s = [score[j:j + 1, :] for j in range(N_EXPERTS)]
    npg = EXPERTS_PER_GROUP
    best = None
    for g in range(N_GROUPS):
        v = r[g * npg:(g + 1) * npg]
        pair_sums = [v[a] + v[b] for a in range(npg) for b in range(a + 1, npg)]
        tg = functools.reduce(jnp.maximum, pair_sums)
        if best is None:
            best, bg = tg, jnp.zeros_like(tg)
        else:
            upd = tg > best
            best = jnp.where(upd, tg, best)
            bg = jnp.where(upd, float(g), bg)

    def of_group(rows, j):
        out = rows[j]
        for g in range(1, N_GROUPS):
            out = jnp.where(bg == float(g), rows[g * npg + j], out)
        return out

    v = [of_group(r, j) for j in range(npg)]
    sv = [of_group(s, j) for j in range(npg)]

    def first_argmax(vals):
        m = functools.reduce(jnp.maximum, vals)
        idx = jnp.full_like(m, float(npg - 1))
        for j in range(npg - 2, -1, -1):
            idx = jnp.where(vals[j] == m, float(j), idx)
        return idx

    i1 = first_argmax(v)
    i2 = first_argmax([jnp.where(i1 == float(j), -jnp.inf, v[j]) for j in range(npg)])
    lo = jnp.minimum(i1, i2)
    hi = jnp.maximum(i1, i2)

    def pick(vals, idx):
        out = vals[0]
        for j in range(1, npg):
            out = jnp.where(idx == float(j), vals[j], out)
        return out

    s_lo, s_hi = pick(sv, lo), pick(sv, hi)
    den = s_lo + s_hi
    pair = lo * (2.0 * npg - 1.0 - lo) * 0.5 + (hi - lo - 1.0)
    cls = bg * float(N_PAIR_CLASSES // N_GROUPS) + pair
    return cls, s_lo / den, s_hi / den


def _outproj_kernel(f_ref, o_ref, x_ref, ga_ref, lg_ref, lb_ref, sh_ref, sc_ref, w_ref, wr_ref, rb_ref, tri_ref,
                    cin_ref, xo_ref, hx_ref, rt_ref, cnt_ref, *, alpha, n_part):
    first = (pl.program_id(0) == 0) & (pl.program_id(1) == 0)

    @pl.when(first)
    def _():
        cnt_ref[...] = cin_ref[...]

    fw = f_ref.shape[2]
    tm = x_ref.shape[1]
    tp = tm // n_part
    logit_parts = []
    packed_parts = []
    for part in range(n_part):
        rows = slice(part * tp, (part + 1) * tp)
        o = jnp.concatenate([o_ref[0, j, rows] for j in range(o_ref.shape[1])], axis=1)
        proj = _dot(f_ref[0, rows], w_ref[:fw, :]) + _dot(o, w_ref[fw:, :])
        z = alpha * x_ref[0, rows] + ga_ref[0] * proj
        x = _ln(z) * lg_ref[...] + lb_ref[...]
        xo_ref[0, rows] = x
        h = _ln(x) * (1.0 + sc_ref[0]) + sh_ref[0]
        hh, hl = _split_bf16(h)
        packed_parts.append(_pack_halves(hh.astype(F32)))
        r = _dot_nt(wr_ref[...], hh)
        r2 = _dot_nt(wr_ref[:N_EXPERTS, :], hl)
        logit_parts.append(r[:N_EXPERTS] + r[N_EXPERTS:] + r2)
    logits = jnp.concatenate(logit_parts, axis=1)
    cls, w0, w1 = _route_rows(logits, rb_ref[...])

    ncls = cnt_ref.shape[0]
    cls_iota = lax.broadcasted_iota(jnp.int32, (ncls, tm), 0).astype(F32)
    onehot = cls_iota == cls
    oh = onehot.astype(F32)
    prefix = _dot(oh.astype(BF16), tri_ref[...])
    counts = cnt_ref[...]
    base = jnp.concatenate([counts] * (tm // LANES), axis=1)
    rank = jnp.sum(jnp.where(onehot, prefix + base, 0.0), axis=0, keepdims=True)
    cnt_ref[...] = counts + jnp.sum(oh, axis=1, keepdims=True)

    row_iota = lax.broadcasted_iota(jnp.int32, (rt_ref.shape[0], tm), 0)
    rt_ref[...] = jnp.where(row_iota == 0, cls, jnp.where(row_iota == 1, rank, 0.0))

    wrow_iota = lax.broadcasted_iota(jnp.int32, (LANES, tm), 0)
    wcols = jnp.where(wrow_iota == 0, w0, jnp.where(wrow_iota == 1, w1, 0.0)).T
    packed = packed_parts[0] if n_part == 1 else jnp.concatenate(packed_parts, axis=0)
    _store_token_tiles(hx_ref, 0, jnp.concatenate([packed, lax.bitcast_convert_type(wcols, jnp.uint32)], axis=1))


def _outproj_call(f, o, x, g_a, ln_g, ln_b, shift, scale, w_out, wr_hilo, router_bias, counts_in, alpha, tm=512):
    bsz, t, d = x.shape
    tm = min(tm, t)
    nt = t // tm
    assert d // 2 + LANES <= ROW_TILE * LANES
    row = lambda w: pl.BlockSpec((1, tm, w), lambda b, i: (b, i, 0))
    vec = pl.BlockSpec((1, 1, d), lambda b, i: (b, 0, 0))
    cst = lambda shp: pl.BlockSpec(shp, lambda b, i: (0,) * len(shp))
    tri = jnp.asarray(np.triu(np.ones((tm, tm), np.float32), 1), BF16)
    return pl.pallas_call(
        functools.partial(_outproj_kernel, alpha=alpha, n_part=1),
        out_shape=(
            jax.ShapeDtypeStruct((bsz, t, d), F32),
            jax.ShapeDtypeStruct((bsz * t * ROW_TILE, LANES), jnp.uint32),
            jax.ShapeDtypeStruct((8, bsz * t), F32),
            jax.ShapeDtypeStruct(counts_in.shape, F32),
        ),
        grid=(bsz, nt),
        in_specs=[row(f.shape[2]), pl.BlockSpec((1, o.shape[1], tm, V_DIM), lambda b, i: (b, 0, i, 0)), row(d),
                  vec, cst((1, d)), cst((1, d)), vec, vec, cst(w_out.shape), cst(wr_hilo.shape),
                  cst((N_EXPERTS, 1)), cst((tm, tm)), cst(counts_in.shape)],
        out_specs=(row(d), pl.BlockSpec((tm * ROW_TILE, LANES), lambda b, i: (b * nt + i, 0)),
                   pl.BlockSpec((8, tm), lambda b, i: (0, b * nt + i)), cst(counts_in.shape)),
        compiler_params=_cparams(("arbitrary", "arbitrary")),
        name="outproj",
    )(f, o, x, g_a, ln_g.reshape(1, d), ln_b.reshape(1, d), shift, scale, w_out, wr_hilo,
      router_bias.reshape(N_EXPERTS, 1).astype(F32), tri, counts_in)


COPY_UNROLL = 8


def _token_copies(n_tok, src_tok, dst_tok, sem):
    def start(g, carry):
        for j in range(COPY_UNROLL):
            r = g * COPY_UNROLL + j
            pltpu.make_async_copy(src_tok(r), dst_tok(r), sem).start(priority=j % 2)
        return carry

    def wait(g, carry):
        for j in range(COPY_UNROLL):
            pltpu.make_async_copy(src_tok(0), dst_tok(0), sem).wait()
        return carry

    lax.fori_loop(0, n_tok // COPY_UNROLL, start, 0)
    lax.fori_loop(0, n_tok // COPY_UNROLL, wait, 0)


def _token_tile(ref, tok):
    return ref.at[pl.ds(pl.multiple_of(tok * ROW_TILE, ROW_TILE), ROW_TILE), :]


def _scatter_tokens_kernel(pos_ref, src_ref, dst_in_ref, dst_ref, sem):
    del dst_in_ref
    _token_copies(src_ref.shape[0] // ROW_TILE, lambda r: _token_tile(src_ref, r),
                  lambda r: _token_tile(dst_ref, pos_ref[0, 0, r]), sem)


def _scatter_tokens_call(src, pos, dst, toks_per_step=2048):
    t = pos.shape[0]
    ts = min(toks_per_step, t)
    return pl.pallas_call(
        _scatter_tokens_kernel,
        out_shape=jax.ShapeDtypeStruct(dst.shape, dst.dtype),
        grid=(t // ts,),
        in_specs=[pl.BlockSpec((1, 1, ts), lambda i: (i, 0, 0), memory_space=pltpu.SMEM),
                  pl.BlockSpec((ts * ROW_TILE, LANES), lambda i: (i, 0)),
                  pl.BlockSpec(memory_space=pl.ANY)],
        out_specs=pl.BlockSpec(memory_space=pl.ANY),
        scratch_shapes=[pltpu.SemaphoreType.DMA(())],
        input_output_aliases={2: 0},
        compiler_params=_cparams(("arbitrary",)),
        name="scatter_tokens",
    )(pos.reshape(t // ts, 1, ts), src, dst)


def _gather_tokens_kernel(pos_ref, src_ref, dst_ref, sem):
    _token_copies(dst_ref.shape[0] // ROW_TILE, lambda r: _token_tile(src_ref, pos_ref[0, 0, r]),
                  lambda r: _token_tile(dst_ref, r), sem)


def _gather_tokens_call(src, pos, toks_per_step=2048):
    t = pos.shape[0]
    ts = min(toks_per_step, t)
    return pl.pallas_call(
        _gather_tokens_kernel,
        out_shape=jax.ShapeDtypeStruct((t * ROW_TILE, LANES), src.dtype),
        grid=(t // ts,),
        in_specs=[pl.BlockSpec((1, 1, ts), lambda i: (i, 0, 0), memory_space=pltpu.SMEM),
                  pl.BlockSpec(memory_space=pl.ANY)],
        out_specs=pl.BlockSpec((ts * ROW_TILE, LANES), lambda i: (i, 0)),
        scratch_shapes=[pltpu.SemaphoreType.DMA(())],
        compiler_params=_cparams(("arbitrary",)),
        name="gather_tokens",
    )(pos.reshape(t // ts, 1, ts), src)


def _moe_kernel(e0_ref, e1_ref, valid_ref, fresh_ref, x_ref, wg0, wu0, wd0, wg1, wu1, wd1, y_ref,
                wg_s, wu_s, wd_s):
    i = pl.program_id(0)
    half = wg_s.shape[1] // 2

    @pl.when(fresh_ref[i] != 0)
    def _():
        for slot, (wg, wu, wd) in enumerate(((wg0, wu0, wd0), (wg1, wu1, wd1))):
            wg_s[slot] = wg[0, 0].astype(BF16)
            wu_s[slot] = wu[0, 0].astype(BF16)
            wd_s[slot] = wd[0, 0].astype(BF16)

    @pl.when(valid_ref[i] != 0)
    def _():
        n_tok = x_ref.shape[0] // ROW_TILE
        xrow = _load_token_tiles(x_ref, 0, n_tok, half // LANES + 1)
        x = _unpack_halves(xrow[:, :half]).astype(BF16)
        wt = lax.bitcast_convert_type(xrow[:, half:], F32)

        def expert(slot):
            g = _dot(x, wg_s[slot])
            u = _dot(x, wu_s[slot])
            a = (g * jax.nn.sigmoid(g) * u).astype(BF16)
            return _dot(a, wd_s[slot])

        y = wt[:, 0:1] * expert(0) + wt[:, 1:2] * expert(1)
        _store_token_tiles(y_ref, 0, _pack_halves(y.astype(BF16).astype(F32)))

    @pl.when(valid_ref[i] == 0)
    def _():
        y_ref[...] = jnp.zeros_like(y_ref)


def _moe_call(xs, tile_e0, tile_e1, tile_valid, tile_fresh, w_gate, w_up, w_down, layer, tmm):
    tpad = xs.shape[0] // ROW_TILE
    _, _, d, de = w_gate.shape
    n_tiles = tpad // tmm
    tok_tiles = pl.BlockSpec((tmm * ROW_TILE, LANES), lambda i, e0, e1, vl, fr: (i, 0))
    wspec0 = lambda shp: pl.BlockSpec((1, 1) + shp, lambda i, e0, e1, vl, fr: (layer, e0[i], 0, 0))
    wspec1 = lambda shp: pl.BlockSpec((1, 1) + shp, lambda i, e0, e1, vl, fr: (layer, e1[i], 0, 0))
    grid_spec = pltpu.PrefetchScalarGridSpec(
        num_scalar_prefetch=4,
        grid=(n_tiles,),
        in_specs=[
            tok_tiles,
            wspec0((d, de)), wspec0((d, de)), wspec0((de, d)),
            wspec1((d, de)), wspec1((d, de)), wspec1((de, d)),
        ],
        out_specs=tok_tiles,
        scratch_shapes=[pltpu.VMEM((2, d, de), BF16), pltpu.VMEM((2, d, de), BF16), pltpu.VMEM((2, de, d), BF16)],
    )
    return pl.pallas_call(
        _moe_kernel,
        out_shape=jax.ShapeDtypeStruct(xs.shape, jnp.uint32),
        grid_spec=grid_spec,
        compiler_params=_cparams(("arbitrary",)),
        name="moe_ffn",
    )(tile_e0, tile_e1, tile_valid, tile_fresh, xs, w_gate, w_up, w_down, w_gate, w_up, w_down)


def _pair_tables():
    e0s, e1s = [], []
    for g in range(N_GROUPS):
        for a in range(EXPERTS_PER_GROUP):
            for b in range(a + 1, EXPERTS_PER_GROUP):
                e0s.append(g * EXPERTS_PER_GROUP + a)
                e1s.append(g * EXPERTS_PER_GROUP + b)
    return np.asarray(e0s, np.int32), np.asarray(e1s, np.int32)


def _moe_ffn(rows, routes, counts, w_gate, w_up, w_down, layer, tmm=512):
    t_all = sum(rt.shape[1] for rt in routes)
    cnt = counts[:N_PAIR_CLASSES, 0].astype(jnp.int32)
    padded = ((cnt + tmm - 1) // tmm) * tmm
    ends = jnp.cumsum(padded)
    starts = ends - padded
    n_tiles = -(-t_all // tmm) + N_PAIR_CLASSES
    tpad = n_tiles * tmm
    total = ends[-1]
    tile_start = jnp.arange(n_tiles, dtype=jnp.int32) * tmm
    tile_valid = (tile_start < total).astype(jnp.int32)
    probe = jnp.minimum(tile_start, total - 1)
    tile_cls = jnp.sum((ends[None, :] <= probe[:, None]).astype(jnp.int32), axis=1)
    tile_cls = jnp.minimum(tile_cls, N_PAIR_CLASSES - 1)
    pe0, pe1 = _pair_tables()
    tile_e0 = jnp.asarray(pe0)[tile_cls]
    tile_e1 = jnp.asarray(pe1)[tile_cls]
    tile_fresh = jnp.concatenate([jnp.ones((1,), jnp.int32),
                                  (tile_cls[1:] != tile_cls[:-1]).astype(jnp.int32)])
    cls_ids = jnp.arange(N_PAIR_CLASSES, dtype=F32)[:, None]
    starts_f = starts.astype(F32)[:, None]
    poss = []
    for rt in routes:
        start_of_tok = jnp.sum(jnp.where(rt[0][None, :] == cls_ids, starts_f, 0.0), axis=0)
        poss.append((start_of_tok + rt[1]).astype(jnp.int32))
    xs = jnp.zeros((tpad * ROW_TILE, LANES), jnp.uint32)
    for r, pos in zip(rows, poss):
        xs = _scatter_tokens_call(r, pos, xs)
    ys = _moe_call(xs, tile_e0, tile_e1, tile_valid, tile_fresh, w_gate, w_up, w_down, layer, tmm)
    return [_gather_tokens_call(ys, pos) for pos in poss]


def _ffn_ln_kernel(x_ref, y_ref, gf_ref, lg_ref, lb_ref, o_ref, *, alpha):
    tm, d = x_ref.shape[1], x_ref.shape[2]
    y = _unpack_halves(_load_token_tiles(y_ref, 0, tm, d // 2 // LANES))
    z = alpha * x_ref[0] + gf_ref[0] * y
    o_ref[0] = _ln(z) * lg_ref[...] + lb_ref[...]


def _ffn_ln_call(x, y, g_f, ln_g, ln_b, alpha, tm=512):
    bsz, t, d = x.shape
    tm = min(tm, t)
    row = pl.BlockSpec((1, tm, d), lambda b, i: (b, i, 0))
    vec = pl.BlockSpec((1, 1, d), lambda b, i: (b, 0, 0))
    cst = pl.BlockSpec((1, d), lambda b, i: (0, 0))
    return pl.pallas_call(
        functools.partial(_ffn_ln_kernel, alpha=alpha),
        out_shape=jax.ShapeDtypeStruct((bsz, t, d), F32),
        grid=(bsz, t // tm),
        in_specs=[row, pl.BlockSpec((tm * ROW_TILE, LANES), lambda b, i: (b * (t // tm) + i, 0)), vec, cst, cst],
        out_specs=row,
        compiler_params=_cparams(("arbitrary", "arbitrary")),
        name="ffn_ln",
    )(x, y, g_f, ln_g.reshape(1, d), ln_b.reshape(1, d))


def _rope_tables(s):
    rows = s // GRID_W
    row = jnp.repeat(jnp.arange(rows), GRID_W).astype(F32)
    col = jnp.tile(jnp.arange(GRID_W), rows).astype(F32)
    freqs = ROPE_BASE ** (-jnp.arange(0, ROPE_AXIS_DIM, 2, dtype=F32) / ROPE_AXIS_DIM)
    ang_row = row[:, None] * freqs
    ang_col = col[:, None] * freqs

    def head(r, c, sign):
        return jnp.concatenate([sign * r, r, sign * c, c], axis=1)

    cos = head(jnp.cos(ang_row), jnp.cos(ang_col), 1.0)
    sin = head(jnp.sin(ang_row), jnp.sin(ang_col), -1.0)
    cos = jnp.concatenate([cos, cos], axis=1)
    sin = jnp.concatenate([sin, sin], axis=1)
    return cos, sin


def kernel(x, c, ctx, c_ctx, w_mod, b_mod, w_in, w_fourier, lam_qk, subln_g, w_out, ln_attn_g, ln_attn_b,
           ln_ffn_g, ln_ffn_b, w_router, router_bias, w_gate, w_up, w_down):
    bsz, s, d = x.shape
    l_ctx = ctx.shape[1]
    depth = w_mod.shape[0]
    alpha = (2 * depth) ** 0.25
    qscale = LOG2E * HEAD_DIM ** -0.5

    pad = (-(bsz + 1)) % 8
    cc = jnp.concatenate([c, c_ctx[None, :], jnp.zeros((pad, d), F32)], axis=0)
    mod = _mod_call(cc, w_mod, b_mod)

    ab = _fprep_call(w_fourier)
    v_off = w_in.shape[2] - (w_in.shape[2] - FOURIER_WIDTH) // 3
    w_uqk = w_in[:, :, :v_off].astype(BF16)
    w_vt = jnp.swapaxes(w_in[:, :, v_off:], 1, 2).astype(BF16)
    w_out_b = w_out.astype(BF16)
    wr_t = w_router.T.astype(F32)
    wr_hi = wr_t.astype(BF16)
    wr_lo = (wr_t - wr_hi.astype(F32)).astype(BF16)
    wr_hilo = jnp.concatenate([wr_hi, wr_lo], axis=0)

    cos, sin = _rope_tables(s)
    tabs_lat = (cos * qscale, sin * qscale, cos, sin)
    ones = jnp.ones((l_ctx, LANES), F32)
    zeros = jnp.zeros((l_ctx, LANES), F32)
    tabs_ctx = (ones * qscale, zeros, ones, zeros)
    dft_lat = _dft_mats(s)
    dft_ctx = _dft_mats(l_ctx)

    xc = ctx
    prev = None
    prev_c = None
    for l in range(depth):
        last = l == depth - 1
        lam_init = 0.8 - 0.6 * math.exp(-0.3 * l)
        lq = lam_qk[l].astype(F32)
        lam = jnp.exp(jnp.sum(lq[0] * lq[1])) - jnp.exp(jnp.sum(lq[2] * lq[3])) + lam_init

        m_lat = mod[l, :bsz].reshape(bsz, 1, 6, d)
        sh_a, sc_a, g_a, sh_f, sc_f, g_f = [m_lat[:, :, i, :] for i in range(6)]
        m_ctx = jnp.broadcast_to(mod[l, bsz].reshape(1, 1, 6, d), (bsz, 1, 6, d))
        csh_a, csc_a, cg_a, csh_f, csc_f, cg_f = [m_ctx[:, :, i, :] for i in range(6)]

        outs = _inproj_call(x, sh_a, sc_a, w_uqk[l], w_vt[l], ab[l], tabs_lat, prev=prev, alpha=alpha)
        u12, q, k, v = outs[:4]
        if prev is not None:
            x = outs[4]
        outs_c = _inproj_call(xc, csh_a, csc_a, w_uqk[l], w_vt[l], ab[l], tabs_ctx, prev=prev_c, alpha=alpha)
        u12c, qc, kc, vc = outs_c[:4]
        if prev_c is not None:
            xc = outs_c[4]

        o_attn = _attn_call(q, [(kc, vc), (k, v)], lam, subln_g[l], lam_init)
        f = _fourier_call(u12, *dft_lat)
        counts = jnp.zeros((32, LANES), F32)
        rows, routes = [], []
        if not last:
            oc_attn = _attn_call(qc, [(kc, vc)], lam, subln_g[l], lam_init)
            fc = _fourier_call(u12c, *dft_ctx)
            xc, hxc, route_c, counts = _outproj_call(fc, oc_attn, xc, cg_a, ln_attn_g[l], ln_attn_b[l], csh_f,
                                                     csc_f, w_out_b[l], wr_hilo, router_bias, counts, alpha)
            rows.append(hxc)
            routes.append(route_c)
        x, hx, route_t, counts = _outproj_call(f, o_attn, x, g_a, ln_attn_g[l], ln_attn_b[l], sh_f, sc_f,
                                               w_out_b[l], wr_hilo, router_bias, counts, alpha)
        rows.append(hx)
        routes.append(route_t)

        ys = _moe_ffn(rows, routes, counts, w_gate, w_up, w_down, l)
        if not last:
            prev_c = (ys[0], cg_f, ln_ffn_g[l], ln_ffn_b[l])
        prev = (ys[-1], g_f, ln_ffn_g[l], ln_ffn_b[l])

    y, g_f, lg, lb = prev
    return _ffn_ln_call(x, y, g_f, lg, lb, alpha)
```

```python
import functools
import math

import jax
import jax.numpy as jnp
import numpy as np
from jax import lax
from jax.experimental import pallas as pl
from jax.experimental.pallas import tpu as pltpu

F32 = jnp.float32
BF16 = jnp.bfloat16

GRID_W = 64
FOURIER_WIDTH = 256
FOURIER_GROUPS = 4
FOURIER_GROUP_DIM = FOURIER_WIDTH // FOURIER_GROUPS
HEAD_DIM = 64
V_DIM = 2 * HEAD_DIM
ROPE_BASE = 10000.0
ROPE_AXIS_DIM = HEAD_DIM // 2
N_EXPERTS = 16
N_GROUPS = 4
EXPERTS_PER_GROUP = N_EXPERTS // N_GROUPS
N_PAIR_CLASSES = N_GROUPS * (EXPERTS_PER_GROUP * (EXPERTS_PER_GROUP - 1) // 2)
LN_EPS = 1e-5
LOG2E = math.log2(math.e)

LANES = 128
SUBLANES_BF16 = 16
VMEM_LIMIT = 48 * 1024 * 1024


def _cparams(sem):
    return pltpu.CompilerParams(dimension_semantics=sem, vmem_limit_bytes=VMEM_LIMIT)


def _split_bf16(a):
    hi = a.astype(BF16)
    lo = (a - hi.astype(F32)).astype(BF16)
    return hi, lo


def _dot(a, b):
    return jnp.dot(a, b, preferred_element_type=F32)


def _dot_nt(a, b):
    return lax.dot_general(a, b, (((1,), (1,)), ((), ())), preferred_element_type=F32)


def _dot3(a, w):
    ah, al = _split_bf16(a)
    wh, wl = _split_bf16(w)
    return _dot(ah, wh) + _dot(ah, wl) + _dot(al, wh)


def _ln(x):
    mu = jnp.mean(x, axis=-1, keepdims=True)
    xc = x - mu
    var = jnp.mean(xc * xc, axis=-1, keepdims=True)
    return xc * lax.rsqrt(var + LN_EPS)


def _mod_kernel(c_ref, w_ref, b_ref, o_ref):
    c = c_ref[...]
    s = c * jax.nn.sigmoid(c)
    o_ref[0] = _dot3(s, w_ref[0]) + b_ref[0]


def _mod_call(cc, w_mod, b_mod):
    depth, d, n = w_mod.shape
    r = cc.shape[0]
    tn = 1536
    return pl.pallas_call(
        _mod_kernel,
        out_shape=jax.ShapeDtypeStruct((depth, r, n), F32),
        grid=(depth, n // tn),
        in_specs=[
            pl.BlockSpec((r, d), lambda l, j: (0, 0)),
            pl.BlockSpec((1, d, tn), lambda l, j: (l, 0, j)),
            pl.BlockSpec((1, 1, tn), lambda l, j: (l, 0, j)),
        ],
        out_specs=pl.BlockSpec((1, r, tn), lambda l, j: (l, 0, j)),
        compiler_params=_cparams(("arbitrary", "arbitrary")),
        name="mod",
    )(cc, w_mod, b_mod.reshape(depth, 1, n))


def _fprep_kernel(c_ref, s_ref, w_ref, a_ref, b_ref):
    w = w_ref[0]
    a_ref[0] = _dot3(c_ref[...], w)
    b_ref[0] = _dot3(s_ref[...], w)


def _fprep_call(w_fourier):
    depth, g, c, _ = w_fourier.shape
    idx = np.arange(c)
    ang = 2.0 * np.pi * ((idx[:, None] * idx[None, :]) % c) / c
    c64 = jnp.asarray(np.cos(ang), F32)
    s64 = jnp.asarray(np.sin(ang), F32)
    wf = w_fourier.reshape(depth * g, c, c)
    spec = pl.BlockSpec((1, c, c), lambda i: (i, 0, 0))
    cst = pl.BlockSpec((c, c), lambda i: (0, 0))
    a, b = pl.pallas_call(
        _fprep_kernel,
        out_shape=(jax.ShapeDtypeStruct(wf.shape, F32),) * 2,
        grid=(depth * g,),
        in_specs=[cst, cst, spec],
        out_specs=(spec, spec),
        compiler_params=_cparams(("arbitrary",)),
        name="fourier_prep",
    )(c64, s64, wf)
    eye = jnp.eye(g, dtype=F32)

    def bd(m):
        m = m.reshape(depth, g, c, c)
        return (m[:, :, :, None, :] * eye[None, :, None, :, None]).reshape(depth, g * c, g * c)

    return jnp.concatenate([bd(a), bd(b)], axis=-1).astype(BF16)


def _inproj_kernel(*refs, fuse_prev, alpha, n_part):
    if fuse_prev:
        (x_ref, y_ref, gf_ref, lg_ref, lb_ref, sh_ref, sc_ref, w_ref, wvt_ref, ab_ref,
         cq_ref, sq_ref, ck_ref, sk_ref, u_ref, q_ref, k_ref, v_ref, xo_ref) = refs
    else:
        (x_ref, sh_ref, sc_ref, w_ref, wvt_ref, ab_ref,
         cq_ref, sq_ref, ck_ref, sk_ref, u_ref, q_ref, k_ref, v_ref) = refs
    fw = FOURIER_WIDTH
    aw = (w_ref.shape[1] - fw) // 2
    nh = aw // LANES
    tp = x_ref.shape[1] // n_part
    lane = lax.broadcasted_iota(jnp.int32, (tp, LANES), 1)
    half = ROPE_AXIS_DIM // 2
    first_half = (lane & half) == 0

    for part in range(n_part):
        rows = slice(part * tp, (part + 1) * tp)
        if fuse_prev:
            y = _unpack_halves(_load_token_tiles(y_ref, part * tp, tp, x_ref.shape[2] // 2 // LANES))
            z = alpha * x_ref[0, rows] + gf_ref[0] * y
            x = _ln(z) * lg_ref[...] + lb_ref[...]
            xo_ref[0, rows] = x
        else:
            x = x_ref[0, rows]
        h = (_ln(x) * (1.0 + sc_ref[0]) + sh_ref[0]).astype(BF16)
        u = _dot(h, w_ref[:, :fw]).astype(BF16)

        def rope_store(p, cos, sin, dst_ref):
            for j in range(nh):
                pj = p[:, j * LANES:(j + 1) * LANES]
                swapped = jnp.where(first_half, pltpu.roll(pj, LANES - half, 1), pltpu.roll(pj, half, 1))
                dst_ref[0, j, rows] = (pj * cos + swapped * sin).astype(BF16)

        rope_store(_dot(h, w_ref[:, fw:fw + aw]), cq_ref[rows], sq_ref[rows], q_ref)
        rope_store(_dot(h, w_ref[:, fw + aw:]), ck_ref[rows], sk_ref[rows], k_ref)
        vt = _dot_nt(wvt_ref[...], h)
        for j in range(nh):
            v_ref[0, j, :, rows] = vt[j * V_DIM:(j + 1) * V_DIM, :].astype(BF16)
        u_ref[0, rows] = _dot(u, ab_ref[...]).astype(BF16)


def _inproj_call(x, shift, scale, w_uqk, w_vt, ab, tabs, prev=None, alpha=1.0, tm=512):
    bsz, t, d = x.shape
    aw = w_vt.shape[0]
    nh = aw // V_DIM
    tm = min(tm, t)
    grid = (bsz, t // tm)
    row = pl.BlockSpec((1, tm, d), lambda b, i: (b, i, 0))
    vec = pl.BlockSpec((1, 1, d), lambda b, i: (b, 0, 0))
    cst = lambda shp: pl.BlockSpec(shp, lambda b, i: (0,) * len(shp))
    tab = pl.BlockSpec((tm, LANES), lambda b, i: (i, 0))
    in_specs, args = [row], [x]
    if prev is not None:
        y, gf, lg, lb = prev
        nt = t // tm
        in_specs += [pl.BlockSpec((tm * ROW_TILE, LANES), lambda b, i: (b * nt + i, 0)), vec, cst((1, d)),
                     cst((1, d))]
        args += [y, gf, lg.reshape(1, d), lb.reshape(1, d)]
    in_specs += [vec, vec, cst(w_uqk.shape), cst(w_vt.shape), cst(ab.shape), tab, tab, tab, tab]
    args += [shift, scale, w_uqk, w_vt, ab, *tabs]
    head_rows = pl.BlockSpec((1, nh, tm, V_DIM), lambda b, i: (b, 0, i, 0))
    out_shape = [
        jax.ShapeDtypeStruct((bsz, t, 2 * FOURIER_WIDTH), BF16),
        jax.ShapeDtypeStruct((bsz, nh, t, V_DIM), BF16),
        jax.ShapeDtypeStruct((bsz, nh, t, V_DIM), BF16),
        jax.ShapeDtypeStruct((bsz, nh, V_DIM, t), BF16),
    ]
    out_specs = [
        pl.BlockSpec((1, tm, 2 * FOURIER_WIDTH), lambda b, i: (b, i, 0)),
        head_rows,
        head_rows,
        pl.BlockSpec((1, nh, V_DIM, tm), lambda b, i: (b, 0, 0, i)),
    ]
    if prev is not None:
        out_shape.append(jax.ShapeDtypeStruct((bsz, t, d), F32))
        out_specs.append(row)
    return pl.pallas_call(
        functools.partial(_inproj_kernel, fuse_prev=prev is not None, alpha=alpha,
                          n_part=2 if tm % (2 * LANES) == 0 else 1),
        out_shape=tuple(out_shape),
        grid=grid,
        in_specs=in_specs,
        out_specs=tuple(out_specs),
        compiler_params=_cparams(("arbitrary", "arbitrary")),
        name="inproj",
    )(*args)


def _attn_kernel(lam_ref, q_ref, *refs, n_seg, norm_scale, tq):
    kv_refs = refs[:2 * n_seg]
    g_ref, o_ref, s_buf, m_buf = refs[2 * n_seg:]
    nh, t = q_ref.shape[1], q_ref.shape[2]
    n_iter = (t // tq) * nh
    lane = lax.broadcasted_iota(jnp.int32, (tq, LANES), 1)
    lam = lam_ref[0]
    gain = g_ref[...] * norm_scale
    seg_len = [kv_refs[2 * j].shape[2] for j in range(n_seg)]
    seg_off = [sum(seg_len[:j]) for j in range(n_seg)]

    def where(it):
        h = lax.rem(it, nh)
        rows = pl.ds(pl.multiple_of(lax.div(it, nh) * tq, tq), tq)
        return h, rows

    def scores(it, sub):
        h, rows = where(it)
        qb = q_ref[0, h, rows, :]
        qs = jnp.where((lane < HEAD_DIM) == (sub == 0), qb, jnp.zeros_like(qb))
        m = None
        for j in range(n_seg):
            s = _dot_nt(kv_refs[2 * j][0, h], qs)
            s_buf[sub, seg_off[j]:seg_off[j] + seg_len[j], :] = s
            mj = jnp.max(s, axis=0, keepdims=True)
            m = mj if m is None else jnp.maximum(m, mj)
        m_buf[sub] = jnp.broadcast_to(m, m_buf.shape[1:])

    def attend(it, sub):
        h, _ = where(it)
        m = m_buf[sub][0:1]
        acc = None
        for j in range(n_seg):
            p = jnp.exp2(s_buf[sub, seg_off[j]:seg_off[j] + seg_len[j], :] - m).astype(BF16)
            vt = kv_refs[2 * j + 1][0, h]
            vext = jnp.concatenate([vt, jnp.ones((SUBLANES_BF16, vt.shape[1]), BF16)], axis=0)
            aj = _dot(vext, p)
            acc = aj if acc is None else acc + aj
        return acc[:V_DIM] / acc[V_DIM:V_DIM + 1]

    def finish(it, o0, o1):
        h, rows = where(it)
        o = o0 - lam * o1
        o = o * lax.rsqrt(jnp.mean(o * o, axis=0, keepdims=True) + LN_EPS)
        o_ref[0, h, rows, :] = (o.T * gain).astype(BF16)

    scores(0, 0)

    def step(it, carry):
        scores(it, 1)
        o0 = attend(it, 0)
        scores(it + 1, 0)
        o1 = attend(it, 1)
        finish(it, o0, o1)
        return carry

    lax.fori_loop(0, n_iter - 1, step, 0)
    last = n_iter - 1
    scores(last, 1)
    o0 = attend(last, 0)
    o1 = attend(last, 1)
    finish(last, o0, o1)


def _attn_call(q, kv_segs, lam, g, lam_init, tq=512):
    bsz, nh, t, _ = q.shape
    tq = min(tq, t)
    whole = lambda a: pl.BlockSpec((1,) + a.shape[1:], lambda b: (b, 0, 0, 0))
    in_specs = [pl.BlockSpec(memory_space=pltpu.SMEM), whole(q)]
    args = [lam.reshape(1).astype(F32), q]
    for k, vt in kv_segs:
        in_specs += [whole(k), whole(vt)]
        args += [k, vt]
    in_specs.append(pl.BlockSpec((1, V_DIM), lambda b: (0, 0)))
    args.append(g.reshape(1, V_DIM).astype(F32))
    n_keys = sum(k.shape[2] for k, _ in kv_segs)
    return pl.pallas_call(
        functools.partial(_attn_kernel, n_seg=len(kv_segs), norm_scale=1.0 - lam_init, tq=tq),
        out_shape=jax.ShapeDtypeStruct(q.shape, BF16),
        grid=(bsz,),
        in_specs=in_specs,
        out_specs=whole(q),
        scratch_shapes=[pltpu.VMEM((2, n_keys, tq), F32), pltpu.VMEM((2, 8, tq), F32)],
        compiler_params=_cparams(("arbitrary",)),
        name="diff_attn",
    )(*args)


def _fourier_kernel(c_ref, s_ref, u1_ref, u2_ref, o_ref):
    o_ref[0] = (_dot(c_ref[...], u1_ref[0]) - _dot(s_ref[...], u2_ref[0])).astype(BF16)


def _dft_mats(t, split=64):
    tt = jnp.arange(t, dtype=jnp.int32)[None, :]
    k1 = jnp.arange(t // split, dtype=jnp.int32)[:, None] * split
    k0 = jnp.arange(split, dtype=jnp.int32)[:, None]
    w = 2.0 * math.pi / t
    a = ((k1 * tt) % t).astype(F32) * w
    b = ((k0 * tt) % t).astype(F32) * w
    ca, sa = jnp.cos(a)[:, None, :], jnp.sin(a)[:, None, :]
    cb, sb = jnp.cos(b)[None, :, :], jnp.sin(b)[None, :, :]
    scale = 1.0 / math.sqrt(t * FOURIER_GROUP_DIM)
    cmat = ((ca * cb - sa * sb) * scale).reshape(t, t)
    smat = ((sa * cb + ca * sb) * scale).reshape(t, t)
    return cmat.astype(BF16), smat.astype(BF16)


def _fourier_call(u12, cmat, smat, tk=1024):
    bsz, t, w2 = u12.shape
    w = w2 // 2
    tk = min(tk, t)
    return pl.pallas_call(
        _fourier_kernel,
        out_shape=jax.ShapeDtypeStruct((bsz, t, w), BF16),
        grid=(t // tk, bsz),
        in_specs=[
            pl.BlockSpec((tk, t), lambda i, b: (i, 0)),
            pl.BlockSpec((tk, t), lambda i, b: (i, 0)),
            pl.BlockSpec((1, t, w), lambda i, b: (b, 0, 0)),
            pl.BlockSpec((1, t, w), lambda i, b: (b, 0, 1)),
        ],
        out_specs=pl.BlockSpec((1, tk, w), lambda i, b: (b, i, 0)),
        compiler_params=_cparams(("arbitrary", "arbitrary")),
        name="fourier_dft",
    )(cmat, smat, u12, u12)


def _pack_halves(v):
    n = v.shape[1] // 2
    bits = lax.bitcast_convert_type(v, jnp.uint32)
    return bits[:, :n] | (bits[:, n:] >> 16)


def _unpack_halves(u):
    hi = lax.bitcast_convert_type(u & jnp.uint32(0xFFFF0000), F32)
    lo = lax.bitcast_convert_type(u << 16, F32)
    return jnp.concatenate([hi, lo], axis=1)


ROW_TILE = 8


def _store_token_tiles(ref, first_tok, words):
    n = words.shape[0]
    k = words.shape[1] // LANES
    for c in range(ROW_TILE):
        chunk = words[:, c * LANES:(c + 1) * LANES] if c < k else jnp.zeros((n, LANES), jnp.uint32)
        ref[pl.ds(first_tok * ROW_TILE + c, n, stride=ROW_TILE), :] = chunk


def _load_token_tiles(ref, first_tok, n, k):
    return jnp.concatenate([ref[pl.ds(first_tok * ROW_TILE + c, n, stride=ROW_TILE), :] for c in range(k)], axis=1)


def _route_rows(logits, bias):
    score = jax.nn.sigmoid(logits)
    sel = score + bias
    r = [sel[j:j + 1, :] for j in range(N_EXPERTS)]
    s = [score[j:j + 1, :] for j in range(N_EXPERTS)]
    npg = EXPERTS_PER_GROUP
    best = None
    for g in range(N_GROUPS):
        v = r[g * npg:(g + 1) * npg]
        pair_sums = [v[a] + v[b] for a in range(npg) for b in range(a + 1, npg)]
        tg = functools.reduce(jnp.maximum, pair_sums)
        if best is None:
            best, bg = tg, jnp.zeros_like(tg)
        else:
            upd = tg > best
            best = jnp.where(upd, tg, best)
            bg = jnp.where(upd, float(g), bg)

    def of_group(rows, j):
        out = rows[j]
        for g in range(1, N_GROUPS):
            out = jnp.where(bg == float(g), rows[g * npg + j], out)
        return out

    v = [of_group(r, j) for j in range(npg)]
    sv = [of_group(s, j) for j in range(npg)]

    def first_argmax(vals):
        m = functools.reduce(jnp.maximum, vals)
        idx = jnp.full_like(m, float(npg - 1))
        for j in range(npg - 2, -1, -1):
            idx = jnp.where(vals[j] == m, float(j), idx)
        return idx

    i1 = first_argmax(v)
    i2 = first_argmax([jnp.where(i1 == float(j), -jnp.inf, v[j]) for j in range(npg)])
    lo = jnp.minimum(i1, i2)
    hi = jnp.maximum(i1, i2)

    def pick(vals, idx):
        out = vals[0]
        for j in range(1, npg):
            out = jnp.where(idx == float(j), vals[j], out)
        return out

    s_lo, s_hi = pick(sv, lo), pick(sv, hi)
    den = s_lo + s_hi
    pair = lo * (2.0 * npg - 1.0 - lo) * 0.5 + (hi - lo - 1.0)
    cls = bg * float(N_PAIR_CLASSES // N_GROUPS) + pair
    return cls, s_lo / den, s_hi / den


def _outproj_kernel(f_ref, o_ref, x_ref, ga_ref, lg_ref, lb_ref, sh_ref, sc_ref, w_ref, wr_ref, rb_ref, tri_ref,
                    cin_ref, xo_ref, hx_ref, rt_ref, cnt_ref, *, alpha, n_part):
    first = (pl.program_id(0) == 0) & (pl.program_id(1) == 0)

    @pl.when(first)
    def _():
        cnt_ref[...] = cin_ref[...]

    fw = f_ref.shape[2]
    tm = x_ref.shape[1]
    tp = tm // n_part
    logit_parts = []
    packed_parts = []
    projs = []
    for part in range(n_part):
        rows = slice(part * tp, (part + 1) * tp)
        o = jnp.concatenate([o_ref[0, j, rows] for j in range(o_ref.shape[1])], axis=1)
        projs.append(_dot(f_ref[0, rows], w_ref[:fw, :]) + _dot(o, w_ref[fw:, :]))
    for part in range(n_part):
        rows = slice(part * tp, (part + 1) * tp)
        z = alpha * x_ref[0, rows] + ga_ref[0] * projs[part]
        x = _ln(z) * lg_ref[...] + lb_ref[...]
        xo_ref[0, rows] = x
        h = _ln(x) * (1.0 + sc_ref[0]) + sh_ref[0]
        hh, hl = _split_bf16(h)
        packed_parts.append(_pack_halves(hh.astype(F32)))
        r = _dot_nt(wr_ref[...], hh)
        r2 = _dot_nt(wr_ref[:N_EXPERTS, :], hl)
        logit_parts.append(r[:N_EXPERTS] + r[N_EXPERTS:] + r2)
    logits = jnp.concatenate(logit_parts, axis=1)
    cls, w0, w1 = _route_rows(logits, rb_ref[...])

    ncls = cnt_ref.shape[0]
    cls_iota = lax.broadcasted_iota(jnp.int32, (ncls, tm), 0).astype(F32)
    onehot = cls_iota == cls
    oh = onehot.astype(F32)
    prefix = _dot(oh.astype(BF16), tri_ref[...])
    counts = cnt_ref[...]
    base = jnp.concatenate([counts] * (tm // LANES), axis=1)
    rank = jnp.sum(jnp.where(onehot, prefix + base, 0.0), axis=0, keepdims=True)
    cnt_ref[...] = counts + jnp.sum(oh, axis=1, keepdims=True)

    row_iota = lax.broadcasted_iota(jnp.int32, (rt_ref.shape[0], tm), 0)
    rt_ref[...] = jnp.where(row_iota == 0, cls, jnp.where(row_iota == 1, rank, 0.0))

    wrow_iota = lax.broadcasted_iota(jnp.int32, (LANES, tm), 0)
    wcols = jnp.where(wrow_iota == 0, w0, jnp.where(wrow_iota == 1, w1, 0.0)).T
    packed = packed_parts[0] if n_part == 1 else jnp.concatenate(packed_parts, axis=0)
    _store_token_tiles(hx_ref, 0, jnp.concatenate([packed, lax.bitcast_convert_type(wcols, jnp.uint32)], axis=1))


def _outproj_call(f, o, x, g_a, ln_g, ln_b, shift, scale, w_out, wr_hilo, router_bias, counts_in, alpha, tm=512):
    bsz, t, d = x.shape
    tm = min(tm, t)
    nt = t // tm
    assert d // 2 + LANES <= ROW_TILE * LANES
    row = lambda w: pl.BlockSpec((1, tm, w), lambda b, i: (b, i, 0))
    vec = pl.BlockSpec((1, 1, d), lambda b, i: (b, 0, 0))
    cst = lambda shp: pl.BlockSpec(shp, lambda b, i: (0,) * len(shp))
    tri = jnp.asarray(np.triu(np.ones((tm, tm), np.float32), 1), BF16)
    return pl.pallas_call(
        functools.partial(_outproj_kernel, alpha=alpha, n_part=2 if tm % (2 * LANES) == 0 else 1),
        out_shape=(
            jax.ShapeDtypeStruct((bsz, t, d), F32),
            jax.ShapeDtypeStruct((bsz * t * ROW_TILE, LANES), jnp.uint32),
            jax.ShapeDtypeStruct((8, bsz * t), F32),
            jax.ShapeDtypeStruct(counts_in.shape, F32),
        ),
        grid=(bsz, nt),
        in_specs=[row(f.shape[2]), pl.BlockSpec((1, o.shape[1], tm, V_DIM), lambda b, i: (b, 0, i, 0)), row(d),
                  vec, cst((1, d)), cst((1, d)), vec, vec, cst(w_out.shape), cst(wr_hilo.shape),
                  cst((N_EXPERTS, 1)), cst((tm, tm)), cst(counts_in.shape)],
        out_specs=(row(d), pl.BlockSpec((tm * ROW_TILE, LANES), lambda b, i: (b * nt + i, 0)),
                   pl.BlockSpec((8, tm), lambda b, i: (0, b * nt + i)), cst(counts_in.shape)),
        compiler_params=_cparams(("arbitrary", "arbitrary")),
        name="outproj",
    )(f, o, x, g_a, ln_g.reshape(1, d), ln_b.reshape(1, d), shift, scale, w_out, wr_hilo,
      router_bias.reshape(N_EXPERTS, 1).astype(F32), tri, counts_in)


COPY_UNROLL = 8


def _token_copies(n_tok, src_tok, dst_tok, sem):
    def start(g, carry):
        for j in range(COPY_UNROLL):
            r = g * COPY_UNROLL + j
            pltpu.make_async_copy(src_tok(r), dst_tok(r), sem).start(priority=j % 2)
        return carry

    def wait(g, carry):
        for j in range(COPY_UNROLL):
            pltpu.make_async_copy(src_tok(0), dst_tok(0), sem).wait()
        return carry

    lax.fori_loop(0, n_tok // COPY_UNROLL, start, 0)
    lax.fori_loop(0, n_tok // COPY_UNROLL, wait, 0)


def _token_tile(ref, tok):
    return ref.at[pl.ds(pl.multiple_of(tok * ROW_TILE, ROW_TILE), ROW_TILE), :]


def _scatter_tokens_kernel(pos_ref, src_ref, dst_in_ref, dst_ref, sem):
    del dst_in_ref
    _token_copies(src_ref.shape[0] // ROW_TILE, lambda r: _token_tile(src_ref, r),
                  lambda r: _token_tile(dst_ref, pos_ref[0, 0, r]), sem)


def _scatter_tokens_call(src, pos, dst, toks_per_step=2048):
    t = pos.shape[0]
    ts = min(toks_per_step, t)
    return pl.pallas_call(
        _scatter_tokens_kernel,
        out_shape=jax.ShapeDtypeStruct(dst.shape, dst.dtype),
        grid=(t // ts,),
        in_specs=[pl.BlockSpec((1, 1, ts), lambda i: (i, 0, 0), memory_space=pltpu.SMEM),
                  pl.BlockSpec((ts * ROW_TILE, LANES), lambda i: (i, 0)),
                  pl.BlockSpec(memory_space=pl.ANY)],
        out_specs=pl.BlockSpec(memory_space=pl.ANY),
        scratch_shapes=[pltpu.SemaphoreType.DMA(())],
        input_output_aliases={2: 0},
        compiler_params=_cparams(("arbitrary",)),
        name="scatter_tokens",
    )(pos.reshape(t // ts, 1, ts), src, dst)


def _zero_tails_kernel(end_ref, dst_ref, zeros_ref, sem):
    zeros_ref[...] = jnp.zeros_like(zeros_ref)
    span = zeros_ref.shape[0] // ROW_TILE

    def tail(c):
        first = jnp.maximum(end_ref[c] - span, 0)
        return dst_ref.at[pl.ds(pl.multiple_of(first * ROW_TILE, ROW_TILE), span * ROW_TILE), :]

    for c in range(end_ref.shape[0]):
        pltpu.make_async_copy(zeros_ref, tail(c), sem).start()
    for c in range(end_ref.shape[0]):
        pltpu.make_async_copy(zeros_ref, tail(c), sem).wait()


def _zero_tails_call(ends, n_slots, span):
    return pl.pallas_call(
        _zero_tails_kernel,
        out_shape=jax.ShapeDtypeStruct((n_slots * ROW_TILE, LANES), jnp.uint32),
        in_specs=[pl.BlockSpec(memory_space=pltpu.SMEM)],
        out_specs=pl.BlockSpec(memory_space=pl.ANY),
        scratch_shapes=[pltpu.VMEM((span * ROW_TILE, LANES), jnp.uint32), pltpu.SemaphoreType.DMA(())],
        compiler_params=_cparams(None),
        name="zero_tails",
    )(ends)


def _gather_tokens_kernel(pos_ref, src_ref, dst_ref, sem):
    _token_copies(dst_ref.shape[0] // ROW_TILE, lambda r: _token_tile(src_ref, pos_ref[0, 0, r]),
                  lambda r: _token_tile(dst_ref, r), sem)


def _gather_tokens_call(src, pos, toks_per_step=2048):
    t = pos.shape[0]
    ts = min(toks_per_step, t)
    return pl.pallas_call(
        _gather_tokens_kernel,
        out_shape=jax.ShapeDtypeStruct((t * ROW_TILE, LANES), src.dtype),
        grid=(t // ts,),
        in_specs=[pl.BlockSpec((1, 1, ts), lambda i: (i, 0, 0), memory_space=pltpu.SMEM),
                  pl.BlockSpec(memory_space=pl.ANY)],
        out_specs=pl.BlockSpec((ts * ROW_TILE, LANES), lambda i: (i, 0)),
        scratch_shapes=[pltpu.SemaphoreType.DMA(())],
        compiler_params=_cparams(("arbitrary",)),
        name="gather_tokens",
    )(pos.reshape(t // ts, 1, ts), src)


def _moe_kernel(e0_ref, e1_ref, src_ref, fresh_ref, x_ref, wg0, wu0, wd0, wg1, wu1, wd1, y_ref,
                wg_s, wu_s, wd_s):
    i = pl.program_id(0)
    half = wg_s.shape[1] // 2
    valid = src_ref[i] == i

    @pl.when(fresh_ref[i] != 0)
    def _():
        for slot, (wg, wu, wd) in enumerate(((wg0, wu0, wd0), (wg1, wu1, wd1))):
            wg_s[slot] = wg[0, 0].astype(BF16)
            wu_s[slot] = wu[0, 0].astype(BF16)
            wd_s[slot] = wd[0, 0].astype(BF16)

    @pl.when(valid)
    def _():
        n_tok = x_ref.shape[0] // ROW_TILE
        xrow = _load_token_tiles(x_ref, 0, n_tok, half // LANES + 1)
        x = _unpack_halves(xrow[:, :half]).astype(BF16)
        wt = lax.bitcast_convert_type(xrow[:, half:], F32)
        gu = [(_dot(x, wg_s[slot]), _dot(x, wu_s[slot])) for slot in range(2)]
        ys = [_dot((g * jax.nn.sigmoid(g) * u).astype(BF16), wd_s[slot]) for slot, (g, u) in enumerate(gu)]
        y = wt[:, 0:1] * ys[0] + wt[:, 1:2] * ys[1]
        _store_token_tiles(y_ref, 0, _pack_halves(y.astype(BF16).astype(F32)))

    @pl.when(jnp.logical_not(valid))
    def _():
        y_ref[...] = jnp.zeros_like(y_ref)


def _moe_call(xs, tile_e0, tile_e1, tile_src, tile_fresh, w_gate, w_up, w_down, layer, tmm):
    tpad = xs.shape[0] // ROW_TILE
    _, _, d, de = w_gate.shape
    n_tiles = tpad // tmm
    tok_tiles = pl.BlockSpec((tmm * ROW_TILE, LANES), lambda i, e0, e1, vl, fr: (i, 0))
    src_tiles = pl.BlockSpec((tmm * ROW_TILE, LANES), lambda i, e0, e1, vl, fr: (vl[i], 0))
    wspec0 = lambda shp: pl.BlockSpec((1, 1) + shp, lambda i, e0, e1, vl, fr: (layer, e0[i], 0, 0))
    wspec1 = lambda shp: pl.BlockSpec((1, 1) + shp, lambda i, e0, e1, vl, fr: (layer, e1[i], 0, 0))
    grid_spec = pltpu.PrefetchScalarGridSpec(
        num_scalar_prefetch=4,
        grid=(n_tiles,),
        in_specs=[
            src_tiles,
            wspec0((d, de)), wspec0((d, de)), wspec0((de, d)),
            wspec1((d, de)), wspec1((d, de)), wspec1((de, d)),
        ],
        out_specs=tok_tiles,
        scratch_shapes=[pltpu.VMEM((2, d, de), BF16), pltpu.VMEM((2, d, de), BF16), pltpu.VMEM((2, de, d), BF16)],
    )
    return pl.pallas_call(
        _moe_kernel,
        out_shape=jax.ShapeDtypeStruct(xs.shape, jnp.uint32),
        grid_spec=grid_spec,
        compiler_params=_cparams(("arbitrary",)),
        name="moe_ffn",
    )(tile_e0, tile_e1, tile_src, tile_fresh, xs, w_gate, w_up, w_down, w_gate, w_up, w_down)


def _pair_tables():
    e0s, e1s = [], []
    for g in range(N_GROUPS):
        for a in range(EXPERTS_PER_GROUP):
            for b in range(a + 1, EXPERTS_PER_GROUP):
                e0s.append(g * EXPERTS_PER_GROUP + a)
                e1s.append(g * EXPERTS_PER_GROUP + b)
    return np.asarray(e0s, np.int32), np.asarray(e1s, np.int32)


def _moe_ffn(rows, routes, counts, w_gate, w_up, w_down, layer, tmm=512):
    t_all = sum(rt.shape[1] for rt in routes)
    cnt = counts[:N_PAIR_CLASSES, 0].astype(jnp.int32)
    padded = ((cnt + tmm - 1) // tmm) * tmm
    ends = jnp.cumsum(padded)
    starts = ends - padded
    n_tiles = -(-t_all // tmm) + N_PAIR_CLASSES
    tpad = n_tiles * tmm
    total = ends[-1]
    tile_start = jnp.arange(n_tiles, dtype=jnp.int32) * tmm
    n_real = total // tmm
    tile_ids = jnp.arange(n_tiles, dtype=jnp.int32)
    tile_src = jnp.where(tile_ids < n_real, tile_ids, n_real - 1)
    probe = jnp.minimum(tile_start, total - 1)
    tile_cls = jnp.sum((ends[None, :] <= probe[:, None]).astype(jnp.int32), axis=1)
    tile_cls = jnp.minimum(tile_cls, N_PAIR_CLASSES - 1)
    pe0, pe1 = _pair_tables()
    tile_e0 = jnp.asarray(pe0)[tile_cls]
    tile_e1 = jnp.asarray(pe1)[tile_cls]
    tile_fresh = jnp.concatenate([jnp.ones((1,), jnp.int32),
                                  (tile_cls[1:] != tile_cls[:-1]).astype(jnp.int32)])
    cls_ids = jnp.arange(N_PAIR_CLASSES, dtype=F32)[:, None]
    starts_f = starts.astype(F32)[:, None]
    poss = []
    for rt in routes:
        start_of_tok = jnp.sum(jnp.where(rt[0][None, :] == cls_ids, starts_f, 0.0), axis=0)
        poss.append((start_of_tok + rt[1]).astype(jnp.int32))
    xs = _zero_tails_call(ends, tpad, tmm)
    for r, pos in zip(rows, poss):
        xs = _scatter_tokens_call(r, pos, xs)
    ys = _moe_call(xs, tile_e0, tile_e1, tile_src, tile_fresh, w_gate, w_up, w_down, layer, tmm)
    return [_gather_tokens_call(ys, pos) for pos in poss]


def _ffn_ln_kernel(x_ref, y_ref, gf_ref, lg_ref, lb_ref, o_ref, *, alpha):
    tm, d = x_ref.shape[1], x_ref.shape[2]
    y = _unpack_halves(_load_token_tiles(y_ref, 0, tm, d // 2 // LANES))
    z = alpha * x_ref[0] + gf_ref[0] * y
    o_ref[0] = _ln(z) * lg_ref[...] + lb_ref[...]


def _ffn_ln_call(x, y, g_f, ln_g, ln_b, alpha, tm=512):
    bsz, t, d = x.shape
    tm = min(tm, t)
    row = pl.BlockSpec((1, tm, d), lambda b, i: (b, i, 0))
    vec = pl.BlockSpec((1, 1, d), lambda b, i: (b, 0, 0))
    cst = pl.BlockSpec((1, d), lambda b, i: (0, 0))
    return pl.pallas_call(
        functools.partial(_ffn_ln_kernel, alpha=alpha),
        out_shape=jax.ShapeDtypeStruct((bsz, t, d), F32),
        grid=(bsz, t // tm),
        in_specs=[row, pl.BlockSpec((tm * ROW_TILE, LANES), lambda b, i: (b * (t // tm) + i, 0)), vec, cst, cst],
        out_specs=row,
        compiler_params=_cparams(("arbitrary", "arbitrary")),
        name="ffn_ln",
    )(x, y, g_f, ln_g.reshape(1, d), ln_b.reshape(1, d))


def _rope_tables(s):
    rows = s // GRID_W
    row = jnp.repeat(jnp.arange(rows), GRID_W).astype(F32)
    col = jnp.tile(jnp.arange(GRID_W), rows).astype(F32)
    freqs = ROPE_BASE ** (-jnp.arange(0, ROPE_AXIS_DIM, 2, dtype=F32) / ROPE_AXIS_DIM)
    ang_row = row[:, None] * freqs
    ang_col = col[:, None] * freqs

    def head(r, c, sign):
        return jnp.concatenate([sign * r, r, sign * c, c], axis=1)

    cos = head(jnp.cos(ang_row), jnp.cos(ang_col), 1.0)
    sin = head(jnp.sin(ang_row), jnp.sin(ang_col), -1.0)
    cos = jnp.concatenate([cos, cos], axis=1)
    sin = jnp.concatenate([sin, sin], axis=1)
    return cos, sin


def kernel(x, c, ctx, c_ctx, w_mod, b_mod, w_in, w_fourier, lam_qk, subln_g, w_out, ln_attn_g, ln_attn_b,
           ln_ffn_g, ln_ffn_b, w_router, router_bias, w_gate, w_up, w_down):
    bsz, s, d = x.shape
    l_ctx = ctx.shape[1]
    depth = w_mod.shape[0]
    alpha = (2 * depth) ** 0.25
    qscale = LOG2E * HEAD_DIM ** -0.5

    pad = (-(bsz + 1)) % 8
    cc = jnp.concatenate([c, c_ctx[None, :], jnp.zeros((pad, d), F32)], axis=0)
    mod = _mod_call(cc, w_mod, b_mod)

    ab = _fprep_call(w_fourier)
    v_off = w_in.shape[2] - (w_in.shape[2] - FOURIER_WIDTH) // 3
    w_uqk = w_in[:, :, :v_off].astype(BF16)
    w_vt = jnp.swapaxes(w_in[:, :, v_off:], 1, 2).astype(BF16)
    w_out_b = w_out.astype(BF16)
    wr_t = w_router.T.astype(F32)
    wr_hi = wr_t.astype(BF16)
    wr_lo = (wr_t - wr_hi.astype(F32)).astype(BF16)
    wr_hilo = jnp.concatenate([wr_hi, wr_lo], axis=0)

    cos, sin = _rope_tables(s)
    tabs_lat = (cos * qscale, sin * qscale, cos, sin)
    ones = jnp.ones((l_ctx, LANES), F32)
    zeros = jnp.zeros((l_ctx, LANES), F32)
    tabs_ctx = (ones * qscale, zeros, ones, zeros)
    dft_lat = _dft_mats(s)
    dft_ctx = _dft_mats(l_ctx)

    xc = ctx
    prev = None
    prev_c = None
    for l in range(depth):
        last = l == depth - 1
        lam_init = 0.8 - 0.6 * math.exp(-0.3 * l)
        lq = lam_qk[l].astype(F32)
        lam = jnp.exp(jnp.sum(lq[0] * lq[1])) - jnp.exp(jnp.sum(lq[2] * lq[3])) + lam_init

        m_lat = mod[l, :bsz].reshape(bsz, 1, 6, d)
        sh_a, sc_a, g_a, sh_f, sc_f, g_f = [m_lat[:, :, i, :] for i in range(6)]
        m_ctx = jnp.broadcast_to(mod[l, bsz].reshape(1, 1, 6, d), (bsz, 1, 6, d))
        csh_a, csc_a, cg_a, csh_f, csc_f, cg_f = [m_ctx[:, :, i, :] for i in range(6)]

        outs = _inproj_call(x, sh_a, sc_a, w_uqk[l], w_vt[l], ab[l], tabs_lat, prev=prev, alpha=alpha)
        u12, q, k, v = outs[:4]
        if prev is not None:
            x = outs[4]
        outs_c = _inproj_call(xc, csh_a, csc_a, w_uqk[l], w_vt[l], ab[l], tabs_ctx, prev=prev_c, alpha=alpha)
        u12c, qc, kc, vc = outs_c[:4]
        if prev_c is not None:
            xc = outs_c[4]

        o_attn = _attn_call(q, [(kc, vc), (k, v)], lam, subln_g[l], lam_init)
        f = _fourier_call(u12, *dft_lat)
        counts = jnp.zeros((32, LANES), F32)
        rows, routes = [], []
        if not last:
            oc_attn = _attn_call(qc, [(kc, vc)], lam, subln_g[l], lam_init)
            fc = _fourier_call(u12c, *dft_ctx)
            xc, hxc, route_c, counts = _outproj_call(fc, oc_attn, xc, cg_a, ln_attn_g[l], ln_attn_b[l], csh_f,
                                                     csc_f, w_out_b[l], wr_hilo, router_bias, counts, alpha)
            rows.append(hxc)
            routes.append(route_c)
        x, hx, route_t, counts = _outproj_call(f, o_attn, x, g_a, ln_attn_g[l], ln_attn_b[l], sh_f, sc_f,
                                               w_out_b[l], wr_hilo, router_bias, counts, alpha)
        rows.append(hx)
        routes.append(route_t)

        ys = _moe_ffn(rows, routes, counts, w_gate, w_up, w_down, l)
        if not last:
            prev_c = (ys[0], cg_f, ln_ffn_g[l], ln_ffn_b[l])
        prev = (ys[-1], g_f, ln_ffn_g[l], ln_ffn_b[l])

    y, g_f, lg, lb = prev
    return _ffn_ln_call(x, y, g_f, lg, lb, alpha)
```

```python
import functools
import math

import jax
import jax.numpy as jnp
import numpy as np
from jax import lax
from jax.experimental import pallas as pl
from jax.experimental.pallas import tpu as pltpu

F32 = jnp.float32
BF16 = jnp.bfloat16

GRID_W = 64
FOURIER_WIDTH = 256
FOURIER_GROUPS = 4
FOURIER_GROUP_DIM = FOURIER_WIDTH // FOURIER_GROUPS
HEAD_DIM = 64
V_DIM = 2 * HEAD_DIM
ROPE_BASE = 10000.0
ROPE_AXIS_DIM = HEAD_DIM // 2
N_EXPERTS = 16
N_GROUPS = 4
EXPERTS_PER_GROUP = N_EXPERTS // N_GROUPS
N_PAIR_CLASSES = N_GROUPS * (EXPERTS_PER_GROUP * (EXPERTS_PER_GROUP - 1) // 2)
LN_EPS = 1e-5
LOG2E = math.log2(math.e)

LANES = 128
SUBLANES_BF16 = 16
VMEM_LIMIT = 48 * 1024 * 1024


def _cparams(sem):
    return pltpu.CompilerParams(dimension_semantics=sem, vmem_limit_bytes=VMEM_LIMIT)


def _split_bf16(a):
    hi = a.astype(BF16)
    lo = (a - hi.astype(F32)).astype(BF16)
    return hi, lo


def _dot(a, b):
    return jnp.dot(a, b, preferred_element_type=F32)


def _dot_nt(a, b):
    return lax.dot_general(a, b, (((1,), (1,)), ((), ())), preferred_element_type=F32)


def _dot3(a, w):
    ah, al = _split_bf16(a)
    wh, wl = _split_bf16(w)
    return _dot(ah, wh) + _dot(ah, wl) + _dot(al, wh)


def _ln(x):
    mu = jnp.mean(x, axis=-1, keepdims=True)
    xc = x - mu
    var = jnp.mean(xc * xc, axis=-1, keepdims=True)
    return xc * lax.rsqrt(var + LN_EPS)


def _mod_kernel(c_ref, w_ref, b_ref, o_ref):
    c = c_ref[...]
    s = c * jax.nn.sigmoid(c)
    o_ref[0] = _dot3(s, w_ref[0]) + b_ref[0]


def _mod_call(cc, w_mod, b_mod):
    depth, d, n = w_mod.shape
    r = cc.shape[0]
    tn = 1536
    return pl.pallas_call(
        _mod_kernel,
        out_shape=jax.ShapeDtypeStruct((depth, r, n), F32),
        grid=(depth, n // tn),
        in_specs=[
            pl.BlockSpec((r, d), lambda l, j: (0, 0)),
            pl.BlockSpec((1, d, tn), lambda l, j: (l, 0, j)),
            pl.BlockSpec((1, 1, tn), lambda l, j: (l, 0, j)),
        ],
        out_specs=pl.BlockSpec((1, r, tn), lambda l, j: (l, 0, j)),
        compiler_params=_cparams(("arbitrary", "arbitrary")),
        name="mod",
    )(cc, w_mod, b_mod.reshape(depth, 1, n))


def _fprep_kernel(c_ref, s_ref, w_ref, a_ref, b_ref):
    w = w_ref[0]
    a_ref[0] = _dot3(c_ref[...], w)
    b_ref[0] = _dot3(s_ref[...], w)


def _fprep_call(w_fourier):
    depth, g, c, _ = w_fourier.shape
    idx = np.arange(c)
    ang = 2.0 * np.pi * ((idx[:, None] * idx[None, :]) % c) / c
    c64 = jnp.asarray(np.cos(ang), F32)
    s64 = jnp.asarray(np.sin(ang), F32)
    wf = w_fourier.reshape(depth * g, c, c)
    spec = pl.BlockSpec((1, c, c), lambda i: (i, 0, 0))
    cst = pl.BlockSpec((c, c), lambda i: (0, 0))
    a, b = pl.pallas_call(
        _fprep_kernel,
        out_shape=(jax.ShapeDtypeStruct(wf.shape, F32),) * 2,
        grid=(depth * g,),
        in_specs=[cst, cst, spec],
        out_specs=(spec, spec),
        compiler_params=_cparams(("arbitrary",)),
        name="fourier_prep",
    )(c64, s64, wf)
    eye = jnp.eye(g, dtype=F32)

    def bd(m):
        m = m.reshape(depth, g, c, c)
        return (m[:, :, :, None, :] * eye[None, :, None, :, None]).reshape(depth, g * c, g * c)

    return jnp.concatenate([bd(a), bd(b)], axis=-1).astype(BF16)


def _inproj_kernel(*refs, fuse_prev, alpha, n_part):
    if fuse_prev:
        (x_ref, y_ref, gf_ref, lg_ref, lb_ref, sh_ref, sc_ref, w_ref, wvt_ref, ab_ref,
         cq_ref, sq_ref, ck_ref, sk_ref, u_ref, q_ref, k_ref, v_ref, xo_ref) = refs
    else:
        (x_ref, sh_ref, sc_ref, w_ref, wvt_ref, ab_ref,
         cq_ref, sq_ref, ck_ref, sk_ref, u_ref, q_ref, k_ref, v_ref) = refs
    fw = FOURIER_WIDTH
    aw = (w_ref.shape[1] - fw) // 2
    nh = aw // LANES
    tp = x_ref.shape[1] // n_part
    lane = lax.broadcasted_iota(jnp.int32, (tp, LANES), 1)
    half = ROPE_AXIS_DIM // 2
    first_half = (lane & half) == 0

    for part in range(n_part):
        rows = slice(part * tp, (part + 1) * tp)
        if fuse_prev:
            y = _unpack_halves(_load_token_tiles(y_ref, part * tp, tp, x_ref.shape[2] // 2 // LANES))
            z = alpha * x_ref[0, rows] + gf_ref[0] * y
            x = _ln(z) * lg_ref[...] + lb_ref[...]
            xo_ref[0, rows] = x
        else:
            x = x_ref[0, rows]
        h = (_ln(x) * (1.0 + sc_ref[0]) + sh_ref[0]).astype(BF16)
        u = _dot(h, w_ref[:, :fw]).astype(BF16)

        def rope_store(p, cos, sin, dst_ref):
            for j in range(nh):
                pj = p[:, j * LANES:(j + 1) * LANES]
                swapped = jnp.where(first_half, pltpu.roll(pj, LANES - half, 1), pltpu.roll(pj, half, 1))
                dst_ref[0, j, rows] = (pj * cos + swapped * sin).astype(BF16)

        rope_store(_dot(h, w_ref[:, fw:fw + aw]), cq_ref[rows], sq_ref[rows], q_ref)
        rope_store(_dot(h, w_ref[:, fw + aw:]), ck_ref[rows], sk_ref[rows], k_ref)
        vt = _dot_nt(wvt_ref[...], h)
        for j in range(nh):
            v_ref[0, j, :, rows] = vt[j * V_DIM:(j + 1) * V_DIM, :].astype(BF16)
        u_ref[0, rows] = _dot(u, ab_ref[...]).astype(BF16)


def _inproj_call(x, shift, scale, w_uqk, w_vt, ab, tabs, prev=None, alpha=1.0, tm=512):
    bsz, t, d = x.shape
    aw = w_vt.shape[0]
    nh = aw // V_DIM
    tm = min(tm, t)
    grid = (bsz, t // tm)
    row = pl.BlockSpec((1, tm, d), lambda b, i: (b, i, 0))
    vec = pl.BlockSpec((1, 1, d), lambda b, i: (b, 0, 0))
    cst = lambda shp: pl.BlockSpec(shp, lambda b, i: (0,) * len(shp))
    tab = pl.BlockSpec((tm, LANES), lambda b, i: (i, 0))
    in_specs, args = [row], [x]
    if prev is not None:
        y, gf, lg, lb = prev
        nt = t // tm
        in_specs += [pl.BlockSpec((tm * ROW_TILE, LANES), lambda b, i: (b * nt + i, 0)), vec, cst((1, d)),
                     cst((1, d))]
        args += [y, gf, lg.reshape(1, d), lb.reshape(1, d)]
    in_specs += [vec, vec, cst(w_uqk.shape), cst(w_vt.shape), cst(ab.shape), tab, tab, tab, tab]
    args += [shift, scale, w_uqk, w_vt, ab, *tabs]
    head_rows = pl.BlockSpec((1, nh, tm, V_DIM), lambda b, i: (b, 0, i, 0))
    out_shape = [
        jax.ShapeDtypeStruct((bsz, t, 2 * FOURIER_WIDTH), BF16),
        jax.ShapeDtypeStruct((bsz, nh, t, V_DIM), BF16),
        jax.ShapeDtypeStruct((bsz, nh, t, V_DIM), BF16),
        jax.ShapeDtypeStruct((bsz, nh, V_DIM, t), BF16),
    ]
    out_specs = [
        pl.BlockSpec((1, tm, 2 * FOURIER_WIDTH), lambda b, i: (b, i, 0)),
        head_rows,
        head_rows,
        pl.BlockSpec((1, nh, V_DIM, tm), lambda b, i: (b, 0, 0, i)),
    ]
    if prev is not None:
        out_shape.append(jax.ShapeDtypeStruct((bsz, t, d), F32))
        out_specs.append(row)
    return pl.pallas_call(
        functools.partial(_inproj_kernel, fuse_prev=prev is not None, alpha=alpha,
                          n_part=2 if tm % (2 * LANES) == 0 else 1),
        out_shape=tuple(out_shape),
        grid=grid,
        in_specs=in_specs,
        out_specs=tuple(out_specs),
        compiler_params=_cparams(("arbitrary", "arbitrary")),
        name="inproj",
    )(*args)


def _attn_kernel(lam_ref, q_ref, *refs, n_seg, norm_scale, tq):
    kv_refs = refs[:2 * n_seg]
    g_ref, o_ref, s_buf, m_buf = refs[2 * n_seg:]
    nh, t = q_ref.shape[1], q_ref.shape[2]
    n_iter = (t // tq) * nh
    lane = lax.broadcasted_iota(jnp.int32, (tq, LANES), 1)
    lam = lam_ref[0]
    gain = g_ref[...] * norm_scale
    seg_len = [kv_refs[2 * j].shape[2] for j in range(n_seg)]
    seg_off = [sum(seg_len[:j]) for j in range(n_seg)]

    def where(it):
        h = lax.rem(it, nh)
        rows = pl.ds(pl.multiple_of(lax.div(it, nh) * tq, tq), tq)
        return h, rows

    def scores(it, sub):
        h, rows = where(it)
        qb = q_ref[0, h, rows, :]
        qs = jnp.where((lane < HEAD_DIM) == (sub == 0), qb, jnp.zeros_like(qb))
        m = None
        for j in range(n_seg):
            s = _dot_nt(kv_refs[2 * j][0, h], qs)
            s_buf[sub, seg_off[j]:seg_off[j] + seg_len[j], :] = s
            mj = jnp.max(s, axis=0, keepdims=True)
            m = mj if m is None else jnp.maximum(m, mj)
        m_buf[sub] = jnp.broadcast_to(m, m_buf.shape[1:])

    def attend(it, sub):
        h, _ = where(it)
        m = m_buf[sub][0:1]
        acc = None
        for j in range(n_seg):
            p = jnp.exp2(s_buf[sub, seg_off[j]:seg_off[j] + seg_len[j], :] - m).astype(BF16)
            vt = kv_refs[2 * j + 1][0, h]
            vext = jnp.concatenate([vt, jnp.ones((SUBLANES_BF16, vt.shape[1]), BF16)], axis=0)
            aj = _dot(vext, p)
            acc = aj if acc is None else acc + aj
        return acc[:V_DIM] / acc[V_DIM:V_DIM + 1]

    def finish(it, o0, o1):
        h, rows = where(it)
        o = o0 - lam * o1
        o = o * lax.rsqrt(jnp.mean(o * o, axis=0, keepdims=True) + LN_EPS)
        o_ref[0, h, rows, :] = (o.T * gain).astype(BF16)

    scores(0, 0)

    def step(it, carry):
        scores(it, 1)
        o0 = attend(it, 0)
        scores(it + 1, 0)
        o1 = attend(it, 1)
        finish(it, o0, o1)
        return carry

    lax.fori_loop(0, n_iter - 1, step, 0)
    last = n_iter - 1
    scores(last, 1)
    o0 = attend(last, 0)
    o1 = attend(last, 1)
    finish(last, o0, o1)


def _attn_call(q, kv_segs, lam, g, lam_init, tq=512):
    bsz, nh, t, _ = q.shape
    tq = min(tq, t)
    whole = lambda a: pl.BlockSpec((1,) + a.shape[1:], lambda b: (b, 0, 0, 0))
    in_specs = [pl.BlockSpec(memory_space=pltpu.SMEM), whole(q)]
    args = [lam.reshape(1).astype(F32), q]
    for k, vt in kv_segs:
        in_specs += [whole(k), whole(vt)]
        args += [k, vt]
    in_specs.append(pl.BlockSpec((1, V_DIM), lambda b: (0, 0)))
    args.append(g.reshape(1, V_DIM).astype(F32))
    n_keys = sum(k.shape[2] for k, _ in kv_segs)
    return pl.pallas_call(
        functools.partial(_attn_kernel, n_seg=len(kv_segs), norm_scale=1.0 - lam_init, tq=tq),
        out_shape=jax.ShapeDtypeStruct(q.shape, BF16),
        grid=(bsz,),
        in_specs=in_specs,
        out_specs=whole(q),
        scratch_shapes=[pltpu.VMEM((2, n_keys, tq), F32), pltpu.VMEM((2, 8, tq), F32)],
        compiler_params=_cparams(("arbitrary",)),
        name="diff_attn",
    )(*args)


def _fourier_kernel(ce_ref, se_ref, co_ref, so_ref, u_ref, o_ref):
    n = u_ref.shape[1] // 2
    w = u_ref.shape[2] // 2
    top = u_ref[0, :n, :].astype(F32)
    bot = u_ref[0, n:, :].astype(F32)
    e = (top + bot).astype(BF16)
    d = (top - bot).astype(BF16)
    even = _dot(ce_ref[...], e[:, :w]) - _dot(se_ref[...], e[:, w:])
    odd = _dot(co_ref[...], d[:, :w]) - _dot(so_ref[...], d[:, w:])
    tk = even.shape[0]
    for half in range(w // LANES):
        lanes = slice(half * LANES, (half + 1) * LANES)
        o_ref[0, half, pl.ds(0, tk, stride=2), :] = even[:, lanes]
        o_ref[0, half, pl.ds(1, tk, stride=2), :] = odd[:, lanes]


def _dft_mats(t, split=64):
    tt = jnp.arange(t, dtype=jnp.int32)[None, :]
    k1 = jnp.arange(t // split, dtype=jnp.int32)[:, None] * split
    k0 = jnp.arange(split, dtype=jnp.int32)[:, None]
    w = 2.0 * math.pi / t
    a = ((k1 * tt) % t).astype(F32) * w
    b = ((k0 * tt) % t).astype(F32) * w
    ca, sa = jnp.cos(a)[:, None, :], jnp.sin(a)[:, None, :]
    cb, sb = jnp.cos(b)[None, :, :], jnp.sin(b)[None, :, :]
    scale = 1.0 / math.sqrt(t * FOURIER_GROUP_DIM)
    cmat = ((ca * cb - sa * sb) * scale).reshape(t, t).astype(BF16)
    smat = ((sa * cb + ca * sb) * scale).reshape(t, t).astype(BF16)
    n = t // 2
    return cmat[0::2, :n], smat[0::2, :n], cmat[1::2, :n], smat[1::2, :n]


def _fourier_call(u12, mats, tk=512):
    bsz, t, w2 = u12.shape
    w = w2 // 2
    n = t // 2
    tk = min(tk, n)
    mat = pl.BlockSpec((tk, n), lambda i, b: (i, 0))
    return pl.pallas_call(
        _fourier_kernel,
        out_shape=jax.ShapeDtypeStruct((bsz, w // LANES, t, LANES), F32),
        grid=(n // tk, bsz),
        in_specs=[mat, mat, mat, mat, pl.BlockSpec((1, t, w2), lambda i, b: (b, 0, 0))],
        out_specs=pl.BlockSpec((1, w // LANES, 2 * tk, LANES), lambda i, b: (b, 0, i, 0)),
        compiler_params=_cparams(("arbitrary", "arbitrary")),
        name="fourier_dft",
    )(*mats, u12)


def _pack_halves(v):
    n = v.shape[1] // 2
    bits = lax.bitcast_convert_type(v, jnp.uint32)
    return bits[:, :n] | (bits[:, n:] >> 16)


def _unpack_halves(u):
    hi = lax.bitcast_convert_type(u & jnp.uint32(0xFFFF0000), F32)
    lo = lax.bitcast_convert_type(u << 16, F32)
    return jnp.concatenate([hi, lo], axis=1)


ROW_TILE = 8


def _store_token_tiles(ref, first_tok, words):
    n = words.shape[0]
    k = words.shape[1] // LANES
    for c in range(ROW_TILE):
        chunk = words[:, c * LANES:(c + 1) * LANES] if c < k else jnp.zeros((n, LANES), jnp.uint32)
        ref[pl.ds(first_tok * ROW_TILE + c, n, stride=ROW_TILE), :] = chunk


def _load_token_tiles(ref, first_tok, n, k):
    return jnp.concatenate([ref[pl.ds(first_tok * ROW_TILE + c, n, stride=ROW_TILE), :] for c in range(k)], axis=1)


def _route_rows(logits, bias):
    score = jax.nn.sigmoid(logits)
    sel = score + bias
    r = [sel[j:j + 1, :] for j in range(N_EXPERTS)]
    s = [score[j:j + 1, :] for j in range(N_EXPERTS)]
    npg = EXPERTS_PER_GROUP
    best = None
    for g in range(N_GROUPS):
        v = r[g * npg:(g + 1) * npg]
        pair_sums = [v[a] + v[b] for a in range(npg) for b in range(a + 1, npg)]
        tg = functools.reduce(jnp.maximum, pair_sums)
        if best is None:
            best, bg = tg, jnp.zeros_like(tg)
        else:
            upd = tg > best
            best = jnp.where(upd, tg, best)
            bg = jnp.where(upd, float(g), bg)

    def of_group(rows, j):
        out = rows[j]
        for g in range(1, N_GROUPS):
            out = jnp.where(bg == float(g), rows[g * npg + j], out)
        return out

    v = [of_group(r, j) for j in range(npg)]
    sv = [of_group(s, j) for j in range(npg)]

    def first_argmax(vals):
        m = functools.reduce(jnp.maximum, vals)
        idx = jnp.full_like(m, float(npg - 1))
        for j in range(npg - 2, -1, -1):
            idx = jnp.where(vals[j] == m, float(j), idx)
        return idx

    i1 = first_argmax(v)
    i2 = first_argmax([jnp.where(i1 == float(j), -jnp.inf, v[j]) for j in range(npg)])
    lo = jnp.minimum(i1, i2)
    hi = jnp.maximum(i1, i2)

    def pick(vals, idx):
        out = vals[0]
        for j in range(1, npg):
            out = jnp.where(idx == float(j), vals[j], out)
        return out

    s_lo, s_hi = pick(sv, lo), pick(sv, hi)
    den = s_lo + s_hi
    pair = lo * (2.0 * npg - 1.0 - lo) * 0.5 + (hi - lo - 1.0)
    cls = bg * float(N_PAIR_CLASSES // N_GROUPS) + pair
    return cls, s_lo / den, s_hi / den


def _outproj_kernel(f_ref, o_ref, x_ref, ga_ref, lg_ref, lb_ref, sh_ref, sc_ref, w_ref, wr_ref, rb_ref, tri_ref,
                    cin_ref, xo_ref, hx_ref, rt_ref, cnt_ref, *, alpha, n_part):
    first = (pl.program_id(0) == 0) & (pl.program_id(1) == 0)

    @pl.when(first)
    def _():
        cnt_ref[...] = cin_ref[...]

    fw = f_ref.shape[1] * f_ref.shape[3]
    tm = x_ref.shape[1]
    tp = tm // n_part
    logit_parts = []
    packed_parts = []
    projs = []
    for part in range(n_part):
        rows = slice(part * tp, (part + 1) * tp)
        o = jnp.concatenate([o_ref[0, j, rows] for j in range(o_ref.shape[1])], axis=1)
        f = jnp.concatenate([f_ref[0, j, rows] for j in range(f_ref.shape[1])], axis=1).astype(BF16)
        projs.append(_dot(f, w_ref[:fw, :]) + _dot(o, w_ref[fw:, :]))
    for part in range(n_part):
        rows = slice(part * tp, (part + 1) * tp)
        z = alpha * x_ref[0, rows] + ga_ref[0] * projs[part]
        x = _ln(z) * lg_ref[...] + lb_ref[...]
        xo_ref[0, rows] = x
        h = _ln(x) * (1.0 + sc_ref[0]) + sh_ref[0]
        hh, hl = _split_bf16(h)
        packed_parts.append(_pack_halves(hh.astype(F32)))
        r = _dot_nt(wr_ref[...], hh)
        r2 = _dot_nt(wr_ref[:N_EXPERTS, :], hl)
        logit_parts.append(r[:N_EXPERTS] + r[N_EXPERTS:] + r2)
    logits = jnp.concatenate(logit_parts, axis=1)
    cls, w0, w1 = _route_rows(logits, rb_ref[...])

    ncls = cnt_ref.shape[0]
    cls_iota = lax.broadcasted_iota(jnp.int32, (ncls, tm), 0).astype(F32)
    onehot = cls_iota == cls
    oh = onehot.astype(F32)
    prefix = _dot(oh.astype(BF16), tri_ref[...])
    counts = cnt_ref[...]
    base = jnp.concatenate([counts] * (tm // LANES), axis=1)
    rank = jnp.sum(jnp.where(onehot, prefix + base, 0.0), axis=0, keepdims=True)
    cnt_ref[...] = counts + jnp.sum(oh, axis=1, keepdims=True)

    row_iota = lax.broadcasted_iota(jnp.int32, (rt_ref.shape[0], tm), 0)
    rt_ref[...] = jnp.where(row_iota == 0, cls, jnp.where(row_iota == 1, rank, 0.0))

    wrow_iota = lax.broadcasted_iota(jnp.int32, (LANES, tm), 0)
    wcols = jnp.where(wrow_iota == 0, w0, jnp.where(wrow_iota == 1, w1, 0.0)).T
    packed = packed_parts[0] if n_part == 1 else jnp.concatenate(packed_parts, axis=0)
    _store_token_tiles(hx_ref, 0, jnp.concatenate([packed, lax.bitcast_convert_type(wcols, jnp.uint32)], axis=1))


def _outproj_call(f, o, x, g_a, ln_g, ln_b, shift, scale, w_out, wr_hilo, router_bias, counts_in, alpha, tm=512):
    bsz, t, d = x.shape
    tm = min(tm, t)
    nt = t // tm
    assert d // 2 + LANES <= ROW_TILE * LANES
    row = lambda w: pl.BlockSpec((1, tm, w), lambda b, i: (b, i, 0))
    vec = pl.BlockSpec((1, 1, d), lambda b, i: (b, 0, 0))
    cst = lambda shp: pl.BlockSpec(shp, lambda b, i: (0,) * len(shp))
    tri = jnp.asarray(np.triu(np.ones((tm, tm), np.float32), 1), BF16)
    return pl.pallas_call(
        functools.partial(_outproj_kernel, alpha=alpha, n_part=2 if tm % (2 * LANES) == 0 else 1),
        out_shape=(
            jax.ShapeDtypeStruct((bsz, t, d), F32),
            jax.ShapeDtypeStruct((bsz * t * ROW_TILE, LANES), jnp.uint32),
            jax.ShapeDtypeStruct((8, bsz * t), F32),
            jax.ShapeDtypeStruct(counts_in.shape, F32),
        ),
        grid=(bsz, nt),
        in_specs=[pl.BlockSpec((1, f.shape[1], tm, LANES), lambda b, i: (b, 0, i, 0)),
                  pl.BlockSpec((1, o.shape[1], tm, V_DIM), lambda b, i: (b, 0, i, 0)), row(d),
                  vec, cst((1, d)), cst((1, d)), vec, vec, cst(w_out.shape), cst(wr_hilo.shape),
                  cst((N_EXPERTS, 1)), cst((tm, tm)), cst(counts_in.shape)],
        out_specs=(row(d), pl.BlockSpec((tm * ROW_TILE, LANES), lambda b, i: (b * nt + i, 0)),
                   pl.BlockSpec((8, tm), lambda b, i: (0, b * nt + i)), cst(counts_in.shape)),
        compiler_params=_cparams(("arbitrary", "arbitrary")),
        name="outproj",
    )(f, o, x, g_a, ln_g.reshape(1, d), ln_b.reshape(1, d), shift, scale, w_out, wr_hilo,
      router_bias.reshape(N_EXPERTS, 1).astype(F32), tri, counts_in)


COPY_UNROLL = 8


def _token_copies(n_tok, src_tok, dst_tok, sem):
    def start(g, carry):
        for j in range(COPY_UNROLL):
            r = g * COPY_UNROLL + j
            pltpu.make_async_copy(src_tok(r), dst_tok(r), sem).start(priority=j % 2)
        return carry

    def wait(g, carry):
        for j in range(COPY_UNROLL):
            pltpu.make_async_copy(src_tok(0), dst_tok(0), sem).wait()
        return carry

    lax.fori_loop(0, n_tok // COPY_UNROLL, start, 0)
    lax.fori_loop(0, n_tok // COPY_UNROLL, wait, 0)


def _token_tile(ref, tok):
    return ref.at[pl.ds(pl.multiple_of(tok * ROW_TILE, ROW_TILE), ROW_TILE), :]


def _scatter_tokens_kernel(pos_ref, src_ref, dst_in_ref, dst_ref, sem):
    del dst_in_ref
    _token_copies(src_ref.shape[0] // ROW_TILE, lambda r: _token_tile(src_ref, r),
                  lambda r: _token_tile(dst_ref, pos_ref[0, 0, r]), sem)


def _scatter_tokens_call(src, pos, dst, toks_per_step=2048):
    t = pos.shape[0]
    ts = min(toks_per_step, t)
    return pl.pallas_call(
        _scatter_tokens_kernel,
        out_shape=jax.ShapeDtypeStruct(dst.shape, dst.dtype),
        grid=(t // ts,),
        in_specs=[pl.BlockSpec((1, 1, ts), lambda i: (i, 0, 0), memory_space=pltpu.SMEM),
                  pl.BlockSpec((ts * ROW_TILE, LANES), lambda i: (i, 0)),
                  pl.BlockSpec(memory_space=pl.ANY)],
        out_specs=pl.BlockSpec(memory_space=pl.ANY),
        scratch_shapes=[pltpu.SemaphoreType.DMA(())],
        input_output_aliases={2: 0},
        compiler_params=_cparams(("arbitrary",)),
        name="scatter_tokens",
    )(pos.reshape(t // ts, 1, ts), src, dst)


def _zero_tails_kernel(end_ref, dst_ref, zeros_ref, sem):
    zeros_ref[...] = jnp.zeros_like(zeros_ref)
    span = zeros_ref.shape[0] // ROW_TILE

    def tail(c):
        first = jnp.maximum(end_ref[c] - span, 0)
        return dst_ref.at[pl.ds(pl.multiple_of(first * ROW_TILE, ROW_TILE), span * ROW_TILE), :]

    for c in range(end_ref.shape[0]):
        pltpu.make_async_copy(zeros_ref, tail(c), sem).start()
    for c in range(end_ref.shape[0]):
        pltpu.make_async_copy(zeros_ref, tail(c), sem).wait()


def _zero_tails_call(ends, n_slots, span):
    return pl.pallas_call(
        _zero_tails_kernel,
        out_shape=jax.ShapeDtypeStruct((n_slots * ROW_TILE, LANES), jnp.uint32),
        in_specs=[pl.BlockSpec(memory_space=pltpu.SMEM)],
        out_specs=pl.BlockSpec(memory_space=pl.ANY),
        scratch_shapes=[pltpu.VMEM((span * ROW_TILE, LANES), jnp.uint32), pltpu.SemaphoreType.DMA(())],
        compiler_params=_cparams(None),
        name="zero_tails",
    )(ends)


def _gather_tokens_kernel(pos_ref, src_ref, dst_ref, sem):
    _token_copies(dst_ref.shape[0] // ROW_TILE, lambda r: _token_tile(src_ref, pos_ref[0, 0, r]),
                  lambda r: _token_tile(dst_ref, r), sem)


def _gather_tokens_call(src, pos, toks_per_step=2048):
    t = pos.shape[0]
    ts = min(toks_per_step, t)
    return pl.pallas_call(
        _gather_tokens_kernel,
        out_shape=jax.ShapeDtypeStruct((t * ROW_TILE, LANES), src.dtype),
        grid=(t // ts,),
        in_specs=[pl.BlockSpec((1, 1, ts), lambda i: (i, 0, 0), memory_space=pltpu.SMEM),
                  pl.BlockSpec(memory_space=pl.ANY)],
        out_specs=pl.BlockSpec((ts * ROW_TILE, LANES), lambda i: (i, 0)),
        scratch_shapes=[pltpu.SemaphoreType.DMA(())],
        compiler_params=_cparams(("arbitrary",)),
        name="gather_tokens",
    )(pos.reshape(t // ts, 1, ts), src)


def _moe_kernel(e0_ref, e1_ref, src_ref, fresh_ref, x_ref, wg0, wu0, wd0, wg1, wu1, wd1, y_ref,
                wg_s, wu_s, wd_s):
    i = pl.program_id(0)
    half = wg_s.shape[1] // 2
    valid = src_ref[i] == i

    @pl.when(fresh_ref[i] != 0)
    def _():
        for slot, (wg, wu, wd) in enumerate(((wg0, wu0, wd0), (wg1, wu1, wd1))):
            wg_s[slot] = wg[0, 0].astype(BF16)
            wu_s[slot] = wu[0, 0].astype(BF16)
            wd_s[slot] = wd[0, 0].astype(BF16)

    @pl.when(valid)
    def _():
        n_tok = x_ref.shape[0] // ROW_TILE
        xrow = _load_token_tiles(x_ref, 0, n_tok, half // LANES + 1)
        x = _unpack_halves(xrow[:, :half]).astype(BF16)
        wt = lax.bitcast_convert_type(xrow[:, half:], F32)
        gu = [(_dot(x, wg_s[slot]), _dot(x, wu_s[slot])) for slot in range(2)]
        ys = [_dot((g * jax.nn.sigmoid(g) * u).astype(BF16), wd_s[slot]) for slot, (g, u) in enumerate(gu)]
        y = wt[:, 0:1] * ys[0] + wt[:, 1:2] * ys[1]
        _store_token_tiles(y_ref, 0, _pack_halves(y.astype(BF16).astype(F32)))

    @pl.when(jnp.logical_not(valid))
    def _():
        y_ref[...] = jnp.zeros_like(y_ref)


def _moe_call(xs, tile_e0, tile_e1, tile_src, tile_fresh, w_gate, w_up, w_down, layer, tmm):
    tpad = xs.shape[0] // ROW_TILE
    _, _, d, de = w_gate.shape
    n_tiles = tpad // tmm
    tok_tiles = pl.BlockSpec((tmm * ROW_TILE, LANES), lambda i, e0, e1, vl, fr: (i, 0))
    src_tiles = pl.BlockSpec((tmm * ROW_TILE, LANES), lambda i, e0, e1, vl, fr: (vl[i], 0))
    wspec0 = lambda shp: pl.BlockSpec((1, 1) + shp, lambda i, e0, e1, vl, fr: (layer, e0[i], 0, 0))
    wspec1 = lambda shp: pl.BlockSpec((1, 1) + shp, lambda i, e0, e1, vl, fr: (layer, e1[i], 0, 0))
    grid_spec = pltpu.PrefetchScalarGridSpec(
        num_scalar_prefetch=4,
        grid=(n_tiles,),
        in_specs=[
            src_tiles,
            wspec0((d, de)), wspec0((d, de)), wspec0((de, d)),
            wspec1((d, de)), wspec1((d, de)), wspec1((de, d)),
        ],
        out_specs=tok_tiles,
        scratch_shapes=[pltpu.VMEM((2, d, de), BF16), pltpu.VMEM((2, d, de), BF16), pltpu.VMEM((2, de, d), BF16)],
    )
    return pl.pallas_call(
        _moe_kernel,
        out_shape=jax.ShapeDtypeStruct(xs.shape, jnp.uint32),
        grid_spec=grid_spec,
        compiler_params=_cparams(("arbitrary",)),
        name="moe_ffn",
    )(tile_e0, tile_e1, tile_src, tile_fresh, xs, w_gate, w_up, w_down, w_gate, w_up, w_down)


def _pair_tables():
    e0s, e1s = [], []
    for g in range(N_GROUPS):
        for a in range(EXPERTS_PER_GROUP):
            for b in range(a + 1, EXPERTS_PER_GROUP):
                e0s.append(g * EXPERTS_PER_GROUP + a)
                e1s.append(g * EXPERTS_PER_GROUP + b)
    return np.asarray(e0s, np.int32), np.asarray(e1s, np.int32)


def _moe_ffn(rows, routes, counts, w_gate, w_up, w_down, layer, tmm=512):
    t_all = sum(rt.shape[1] for rt in routes)
    cnt = counts[:N_PAIR_CLASSES, 0].astype(jnp.int32)
    padded = ((cnt + tmm - 1) // tmm) * tmm
    ends = jnp.cumsum(padded)
    starts = ends - padded
    n_tiles = -(-t_all // tmm) + N_PAIR_CLASSES
    tpad = n_tiles * tmm
    total = ends[-1]
    tile_start = jnp.arange(n_tiles, dtype=jnp.int32) * tmm
    n_real = total // tmm
    tile_ids = jnp.arange(n_tiles, dtype=jnp.int32)
    tile_src = jnp.where(tile_ids < n_real, tile_ids, n_real - 1)
    probe = jnp.minimum(tile_start, total - 1)
    tile_cls = jnp.sum((ends[None, :] <= probe[:, None]).astype(jnp.int32), axis=1)
    tile_cls = jnp.minimum(tile_cls, N_PAIR_CLASSES - 1)
    pe0, pe1 = _pair_tables()
    tile_e0 = jnp.asarray(pe0)[tile_cls]
    tile_e1 = jnp.asarray(pe1)[tile_cls]
    tile_fresh = jnp.concatenate([jnp.ones((1,), jnp.int32),
                                  (tile_cls[1:] != tile_cls[:-1]).astype(jnp.int32)])
    cls_ids = jnp.arange(N_PAIR_CLASSES, dtype=F32)[:, None]
    starts_f = starts.astype(F32)[:, None]
    poss = []
    for rt in routes:
        start_of_tok = jnp.sum(jnp.where(rt[0][None, :] == cls_ids, starts_f, 0.0), axis=0)
        poss.append((start_of_tok + rt[1]).astype(jnp.int32))
    xs = _zero_tails_call(ends, tpad, tmm)
    for r, pos in zip(rows, poss):
        xs = _scatter_tokens_call(r, pos, xs)
    ys = _moe_call(xs, tile_e0, tile_e1, tile_src, tile_fresh, w_gate, w_up, w_down, layer, tmm)
    return [_gather_tokens_call(ys, pos) for pos in poss]


def _ffn_ln_kernel(x_ref, y_ref, gf_ref, lg_ref, lb_ref, o_ref, *, alpha):
    tm, d = x_ref.shape[1], x_ref.shape[2]
    y = _unpack_halves(_load_token_tiles(y_ref, 0, tm, d // 2 // LANES))
    z = alpha * x_ref[0] + gf_ref[0] * y
    o_ref[0] = _ln(z) * lg_ref[...] + lb_ref[...]


def _ffn_ln_call(x, y, g_f, ln_g, ln_b, alpha, tm=512):
    bsz, t, d = x.shape
    tm = min(tm, t)
    row = pl.BlockSpec((1, tm, d), lambda b, i: (b, i, 0))
    vec = pl.BlockSpec((1, 1, d), lambda b, i: (b, 0, 0))
    cst = pl.BlockSpec((1, d), lambda b, i: (0, 0))
    return pl.pallas_call(
        functools.partial(_ffn_ln_kernel, alpha=alpha),
        out_shape=jax.ShapeDtypeStruct((bsz, t, d), F32),
        grid=(bsz, t // tm),
        in_specs=[row, pl.BlockSpec((tm * ROW_TILE, LANES), lambda b, i: (b * (t // tm) + i, 0)), vec, cst, cst],
        out_specs=row,
        compiler_params=_cparams(("arbitrary", "arbitrary")),
        name="ffn_ln",
    )(x, y, g_f, ln_g.reshape(1, d), ln_b.reshape(1, d))


def _rope_tables(s):
    rows = s // GRID_W
    row = jnp.repeat(jnp.arange(rows), GRID_W).astype(F32)
    col = jnp.tile(jnp.arange(GRID_W), rows).astype(F32)
    freqs = ROPE_BASE ** (-jnp.arange(0, ROPE_AXIS_DIM, 2, dtype=F32) / ROPE_AXIS_DIM)
    ang_row = row[:, None] * freqs
    ang_col = col[:, None] * freqs

    def head(r, c, sign):
        return jnp.concatenate([sign * r, r, sign * c, c], axis=1)

    cos = head(jnp.cos(ang_row), jnp.cos(ang_col), 1.0)
    sin = head(jnp.sin(ang_row), jnp.sin(ang_col), -1.0)
    cos = jnp.concatenate([cos, cos], axis=1)
    sin = jnp.concatenate([sin, sin], axis=1)
    return cos, sin


def kernel(x, c, ctx, c_ctx, w_mod, b_mod, w_in, w_fourier, lam_qk, subln_g, w_out, ln_attn_g, ln_attn_b,
           ln_ffn_g, ln_ffn_b, w_router, router_bias, w_gate, w_up, w_down):
    bsz, s, d = x.shape
    l_ctx = ctx.shape[1]
    depth = w_mod.shape[0]
    alpha = (2 * depth) ** 0.25
    qscale = LOG2E * HEAD_DIM ** -0.5

    pad = (-(bsz + 1)) % 8
    cc = jnp.concatenate([c, c_ctx[None, :], jnp.zeros((pad, d), F32)], axis=0)
    mod = _mod_call(cc, w_mod, b_mod)

    ab = _fprep_call(w_fourier)
    v_off = w_in.shape[2] - (w_in.shape[2] - FOURIER_WIDTH) // 3
    w_uqk = w_in[:, :, :v_off].astype(BF16)
    w_vt = jnp.swapaxes(w_in[:, :, v_off:], 1, 2).astype(BF16)
    w_out_b = w_out.astype(BF16)
    wr_t = w_router.T.astype(F32)
    wr_hi = wr_t.astype(BF16)
    wr_lo = (wr_t - wr_hi.astype(F32)).astype(BF16)
    wr_hilo = jnp.concatenate([wr_hi, wr_lo], axis=0)

    cos, sin = _rope_tables(s)
    tabs_lat = (cos * qscale, sin * qscale, cos, sin)
    ones = jnp.ones((l_ctx, LANES), F32)
    zeros = jnp.zeros((l_ctx, LANES), F32)
    tabs_ctx = (ones * qscale, zeros, ones, zeros)
    dft_lat = _dft_mats(s)
    dft_ctx = _dft_mats(l_ctx)

    xc = ctx
    prev = None
    prev_c = None
    for l in range(depth):
        last = l == depth - 1
        lam_init = 0.8 - 0.6 * math.exp(-0.3 * l)
        lq = lam_qk[l].astype(F32)
        lam = jnp.exp(jnp.sum(lq[0] * lq[1])) - jnp.exp(jnp.sum(lq[2] * lq[3])) + lam_init

        m_lat = mod[l, :bsz].reshape(bsz, 1, 6, d)
        sh_a, sc_a, g_a, sh_f, sc_f, g_f = [m_lat[:, :, i, :] for i in range(6)]
        m_ctx = jnp.broadcast_to(mod[l, bsz].reshape(1, 1, 6, d), (bsz, 1, 6, d))
        csh_a, csc_a, cg_a, csh_f, csc_f, cg_f = [m_ctx[:, :, i, :] for i in range(6)]

        outs = _inproj_call(x, sh_a, sc_a, w_uqk[l], w_vt[l], ab[l], tabs_lat, prev=prev, alpha=alpha)
        u12, q, k, v = outs[:4]
        if prev is not None:
            x = outs[4]
        outs_c = _inproj_call(xc, csh_a, csc_a, w_uqk[l], w_vt[l], ab[l], tabs_ctx, prev=prev_c, alpha=alpha)
        u12c, qc, kc, vc = outs_c[:4]
        if prev_c is not None:
            xc = outs_c[4]

        o_attn = _attn_call(q, [(kc, vc), (k, v)], lam, subln_g[l], lam_init)
        f = _fourier_call(u12, dft_lat)
        counts = jnp.zeros((32, LANES), F32)
        rows, routes = [], []
        if not last:
            oc_attn = _attn_call(qc, [(kc, vc)], lam, subln_g[l], lam_init)
            fc = _fourier_call(u12c, dft_ctx)
            xc, hxc, route_c, counts = _outproj_call(fc, oc_attn, xc, cg_a, ln_attn_g[l], ln_attn_b[l], csh_f,
                                                     csc_f, w_out_b[l], wr_hilo, router_bias, counts, alpha)
            rows.append(hxc)
            routes.append(route_c)
        x, hx, route_t, counts = _outproj_call(f, o_attn, x, g_a, ln_attn_g[l], ln_attn_b[l], sh_f, sc_f,
                                               w_out_b[l], wr_hilo, router_bias, counts, alpha)
        rows.append(hx)
        routes.append(route_t)

        ys = _moe_ffn(rows, routes, counts, w_gate, w_up, w_down, l)
        if not last:
            prev_c = (ys[0], cg_f, ln_ffn_g[l], ln_ffn_b[l])
        prev = (ys[-1], g_f, ln_ffn_g[l], ln_ffn_b[l])

    y, g_f, lg, lb = prev
    return _ffn_ln_call(x, y, g_f, lg, lb, alpha)
```

```python
import functools
import math

import jax
import jax.numpy as jnp
import numpy as np
from jax import lax
from jax.experimental import pallas as pl
from jax.experimental.pallas import tpu as pltpu

F32 = jnp.float32
BF16 = jnp.bfloat16

GRID_W = 64
FOURIER_WIDTH = 256
FOURIER_GROUPS = 4
FOURIER_GROUP_DIM = FOURIER_WIDTH // FOURIER_GROUPS
HEAD_DIM = 64
V_DIM = 2 * HEAD_DIM
ROPE_BASE = 10000.0
ROPE_AXIS_DIM = HEAD_DIM // 2
N_EXPERTS = 16
N_GROUPS = 4
EXPERTS_PER_GROUP = N_EXPERTS // N_GROUPS
N_PAIR_CLASSES = N_GROUPS * (EXPERTS_PER_GROUP * (EXPERTS_PER_GROUP - 1) // 2)
LN_EPS = 1e-5
LOG2E = math.log2(math.e)

LANES = 128
SUBLANES_BF16 = 16
VMEM_LIMIT = 48 * 1024 * 1024


def _cparams(sem):
    return pltpu.CompilerParams(dimension_semantics=sem, vmem_limit_bytes=VMEM_LIMIT)


def _split_bf16(a):
    hi = a.astype(BF16)
    lo = (a - hi.astype(F32)).astype(BF16)
    return hi, lo


def _dot(a, b):
    return jnp.dot(a, b, preferred_element_type=F32)


def _dot_nt(a, b):
    return lax.dot_general(a, b, (((1,), (1,)), ((), ())), preferred_element_type=F32)


def _dot3(a, w):
    ah, al = _split_bf16(a)
    wh, wl = _split_bf16(w)
    return _dot(ah, wh) + _dot(ah, wl) + _dot(al, wh)


def _ln(x):
    mu = jnp.mean(x, axis=-1, keepdims=True)
    xc = x - mu
    var = jnp.mean(xc * xc, axis=-1, keepdims=True)
    return xc * lax.rsqrt(var + LN_EPS)


def _mod_kernel(c_ref, w_ref, b_ref, o_ref):
    c = c_ref[...]
    s = c * jax.nn.sigmoid(c)
    o_ref[0] = _dot3(s, w_ref[0]) + b_ref[0]


def _mod_call(cc, w_mod, b_mod):
    depth, d, n = w_mod.shape
    r = cc.shape[0]
    tn = 1536
    return pl.pallas_call(
        _mod_kernel,
        out_shape=jax.ShapeDtypeStruct((depth, r, n), F32),
        grid=(depth, n // tn),
        in_specs=[
            pl.BlockSpec((r, d), lambda l, j: (0, 0)),
            pl.BlockSpec((1, d, tn), lambda l, j: (l, 0, j)),
            pl.BlockSpec((1, 1, tn), lambda l, j: (l, 0, j)),
        ],
        out_specs=pl.BlockSpec((1, r, tn), lambda l, j: (l, 0, j)),
        compiler_params=_cparams(("arbitrary", "arbitrary")),
        name="mod",
    )(cc, w_mod, b_mod.reshape(depth, 1, n))


def _fprep_kernel(c_ref, s_ref, w_ref, a_ref, b_ref):
    w = w_ref[0]
    a_ref[0] = _dot3(c_ref[...], w)
    b_ref[0] = _dot3(s_ref[...], w)


def _fprep_call(w_fourier):
    depth, g, c, _ = w_fourier.shape
    idx = np.arange(c)
    ang = 2.0 * np.pi * ((idx[:, None] * idx[None, :]) % c) / c
    c64 = jnp.asarray(np.cos(ang), F32)
    s64 = jnp.asarray(np.sin(ang), F32)
    wf = w_fourier.reshape(depth * g, c, c)
    spec = pl.BlockSpec((1, c, c), lambda i: (i, 0, 0))
    cst = pl.BlockSpec((c, c), lambda i: (0, 0))
    a, b = pl.pallas_call(
        _fprep_kernel,
        out_shape=(jax.ShapeDtypeStruct(wf.shape, F32),) * 2,
        grid=(depth * g,),
        in_specs=[cst, cst, spec],
        out_specs=(spec, spec),
        compiler_params=_cparams(("arbitrary",)),
        name="fourier_prep",
    )(c64, s64, wf)
    eye = jnp.eye(g, dtype=F32)

    def bd(m):
        m = m.reshape(depth, g, c, c)
        return (m[:, :, :, None, :] * eye[None, :, None, :, None]).reshape(depth, g * c, g * c)

    return jnp.concatenate([bd(a), bd(b)], axis=-1).astype(BF16)


def _inproj_kernel(*refs, fuse_prev, alpha, n_part):
    if fuse_prev:
        (x_ref, y_ref, gf_ref, lg_ref, lb_ref, sh_ref, sc_ref, w_ref, wvt_ref, ab_ref,
         cq_ref, sq_ref, ck_ref, sk_ref, u_ref, q_ref, k_ref, v_ref, xo_ref) = refs
    else:
        (x_ref, sh_ref, sc_ref, w_ref, wvt_ref, ab_ref,
         cq_ref, sq_ref, ck_ref, sk_ref, u_ref, q_ref, k_ref, v_ref) = refs
    fw = FOURIER_WIDTH
    aw = (w_ref.shape[1] - fw) // 2
    nh = aw // LANES
    tp = x_ref.shape[1] // n_part
    lane = lax.broadcasted_iota(jnp.int32, (tp, LANES), 1)
    half = ROPE_AXIS_DIM // 2
    first_half = (lane & half) == 0

    for part in range(n_part):
        rows = slice(part * tp, (part + 1) * tp)
        if fuse_prev:
            y = _unpack_halves(_load_token_tiles(y_ref, part * tp, tp, x_ref.shape[2] // 2 // LANES))
            z = alpha * x_ref[0, rows] + gf_ref[0] * y
            x = _ln(z) * lg_ref[...] + lb_ref[...]
            xo_ref[0, rows] = x
        else:
            x = x_ref[0, rows]
        h = (_ln(x) * (1.0 + sc_ref[0]) + sh_ref[0]).astype(BF16)
        u = _dot(h, w_ref[:, :fw]).astype(BF16)

        def rope_store(p, cos, sin, dst_ref):
            for j in range(nh):
                pj = p[:, j * LANES:(j + 1) * LANES]
                swapped = jnp.where(first_half, pltpu.roll(pj, LANES - half, 1), pltpu.roll(pj, half, 1))
                dst_ref[0, j, rows] = (pj * cos + swapped * sin).astype(BF16)

        rope_store(_dot(h, w_ref[:, fw:fw + aw]), cq_ref[rows], sq_ref[rows], q_ref)
        rope_store(_dot(h, w_ref[:, fw + aw:]), ck_ref[rows], sk_ref[rows], k_ref)
        vt = _dot_nt(wvt_ref[...], h)
        for j in range(nh):
            v_ref[0, j, :, rows] = vt[j * V_DIM:(j + 1) * V_DIM, :].astype(BF16)
        u_ref[0, rows] = _dot(u, ab_ref[...]).astype(BF16)


def _inproj_call(x, shift, scale, w_uqk, w_vt, ab, tabs, prev=None, alpha=1.0, tm=512):
    bsz, t, d = x.shape
    aw = w_vt.shape[0]
    nh = aw // V_DIM
    tm = min(tm, t)
    grid = (bsz, t // tm)
    row = pl.BlockSpec((1, tm, d), lambda b, i: (b, i, 0))
    vec = pl.BlockSpec((1, 1, d), lambda b, i: (b, 0, 0))
    cst = lambda shp: pl.BlockSpec(shp, lambda b, i: (0,) * len(shp))
    tab = pl.BlockSpec((tm, LANES), lambda b, i: (i, 0))
    in_specs, args = [row], [x]
    if prev is not None:
        y, gf, lg, lb = prev
        nt = t // tm
        in_specs += [pl.BlockSpec((tm * ROW_TILE, LANES), lambda b, i: (b * nt + i, 0)), vec, cst((1, d)),
                     cst((1, d))]
        args += [y, gf, lg.reshape(1, d), lb.reshape(1, d)]
    in_specs += [vec, vec, cst(w_uqk.shape), cst(w_vt.shape), cst(ab.shape), tab, tab, tab, tab]
    args += [shift, scale, w_uqk, w_vt, ab, *tabs]
    head_rows = pl.BlockSpec((1, nh, tm, V_DIM), lambda b, i: (b, 0, i, 0))
    out_shape = [
        jax.ShapeDtypeStruct((bsz, t, 2 * FOURIER_WIDTH), BF16),
        jax.ShapeDtypeStruct((bsz, nh, t, V_DIM), BF16),
        jax.ShapeDtypeStruct((bsz, nh, t, V_DIM), BF16),
        jax.ShapeDtypeStruct((bsz, nh, V_DIM, t), BF16),
    ]
    out_specs = [
        pl.BlockSpec((1, tm, 2 * FOURIER_WIDTH), lambda b, i: (b, i, 0)),
        head_rows,
        head_rows,
        pl.BlockSpec((1, nh, V_DIM, tm), lambda b, i: (b, 0, 0, i)),
    ]
    if prev is not None:
        out_shape.append(jax.ShapeDtypeStruct((bsz, t, d), F32))
        out_specs.append(row)
    return pl.pallas_call(
        functools.partial(_inproj_kernel, fuse_prev=prev is not None, alpha=alpha,
                          n_part=2 if tm % (2 * LANES) == 0 else 1),
        out_shape=tuple(out_shape),
        grid=grid,
        in_specs=in_specs,
        out_specs=tuple(out_specs),
        compiler_params=_cparams(("arbitrary", "arbitrary")),
        name="inproj",
    )(*args)


def _attn_kernel(lam_ref, q_ref, *refs, n_seg, norm_scale, tq):
    kv_refs = refs[:2 * n_seg]
    g_ref, o_ref, s_buf, m_buf = refs[2 * n_seg:]
    nh, t = q_ref.shape[1], q_ref.shape[2]
    n_iter = (t // tq) * nh
    lane = lax.broadcasted_iota(jnp.int32, (tq, LANES), 1)
    lam = lam_ref[0]
    gain = g_ref[...] * norm_scale
    seg_len = [kv_refs[2 * j].shape[2] for j in range(n_seg)]
    seg_off = [sum(seg_len[:j]) for j in range(n_seg)]

    def where(it):
        h = lax.rem(it, nh)
        rows = pl.ds(pl.multiple_of(lax.div(it, nh) * tq, tq), tq)
        return h, rows

    def scores(it, sub):
        h, rows = where(it)
        qb = q_ref[0, h, rows, :]
        qs = jnp.where((lane < HEAD_DIM) == (sub == 0), qb, jnp.zeros_like(qb))
        m = None
        for j in range(n_seg):
            s = _dot_nt(kv_refs[2 * j][0, h], qs)
            s_buf[sub, seg_off[j]:seg_off[j] + seg_len[j], :] = s
            mj = jnp.max(s, axis=0, keepdims=True)
            m = mj if m is None else jnp.maximum(m, mj)
        m_buf[sub] = jnp.broadcast_to(m, m_buf.shape[1:])

    def attend(it, sub):
        h, _ = where(it)
        m = m_buf[sub][0:1]
        acc = None
        for j in range(n_seg):
            p = jnp.exp2(s_buf[sub, seg_off[j]:seg_off[j] + seg_len[j], :] - m).astype(BF16)
            vt = kv_refs[2 * j + 1][0, h]
            vext = jnp.concatenate([vt, jnp.ones((SUBLANES_BF16, vt.shape[1]), BF16)], axis=0)
            aj = _dot(vext, p)
            acc = aj if acc is None else acc + aj
        return acc[:V_DIM] / acc[V_DIM:V_DIM + 1]

    def finish(it, o0, o1):
        h, rows = where(it)
        o = o0 - lam * o1
        o = o * lax.rsqrt(jnp.mean(o * o, axis=0, keepdims=True) + LN_EPS)
        o_ref[0, h, rows, :] = (o.T * gain).astype(BF16)

    scores(0, 0)

    def step(it, carry):
        scores(it, 1)
        o0 = attend(it, 0)
        scores(it + 1, 0)
        o1 = attend(it, 1)
        finish(it, o0, o1)
        return carry

    lax.fori_loop(0, n_iter - 1, step, 0)
    last = n_iter - 1
    scores(last, 1)
    o0 = attend(last, 0)
    o1 = attend(last, 1)
    finish(last, o0, o1)


def _attn_call(q, kv_segs, lam, g, lam_init, tq=512):
    bsz, nh, t, _ = q.shape
    tq = min(tq, t)
    whole = lambda a: pl.BlockSpec((1,) + a.shape[1:], lambda b: (b, 0, 0, 0))
    in_specs = [pl.BlockSpec(memory_space=pltpu.SMEM), whole(q)]
    args = [lam.reshape(1).astype(F32), q]
    for k, vt in kv_segs:
        in_specs += [whole(k), whole(vt)]
        args += [k, vt]
    in_specs.append(pl.BlockSpec((1, V_DIM), lambda b: (0, 0)))
    args.append(g.reshape(1, V_DIM).astype(F32))
    n_keys = sum(k.shape[2] for k, _ in kv_segs)
    return pl.pallas_call(
        functools.partial(_attn_kernel, n_seg=len(kv_segs), norm_scale=1.0 - lam_init, tq=tq),
        out_shape=jax.ShapeDtypeStruct(q.shape, BF16),
        grid=(bsz,),
        in_specs=in_specs,
        out_specs=whole(q),
        scratch_shapes=[pltpu.VMEM((2, n_keys, tq), F32), pltpu.VMEM((2, 8, tq), F32)],
        compiler_params=_cparams(("arbitrary",)),
        name="diff_attn",
    )(*args)


def _fourier_kernel(ce_ref, se_ref, co_ref, so_ref, u_ref, o_ref):
    n = u_ref.shape[1] // 2
    w = u_ref.shape[2] // 2
    top = u_ref[0, :n, :].astype(F32)
    bot = u_ref[0, n:, :].astype(F32)
    e = (top + bot).astype(BF16)
    d = (top - bot).astype(BF16)
    even = _dot(ce_ref[...], e[:, :w]) - _dot(se_ref[...], e[:, w:])
    odd = _dot(co_ref[...], d[:, :w]) - _dot(so_ref[...], d[:, w:])
    tk = even.shape[0]
    for half in range(w // LANES):
        lanes = slice(half * LANES, (half + 1) * LANES)
        o_ref[0, half, pl.ds(0, tk, stride=2), :] = even[:, lanes]
        o_ref[0, half, pl.ds(1, tk, stride=2), :] = odd[:, lanes]


def _dft_mats(t, split=64):
    n = t // 2
    split = min(split, n)
    tt = jnp.arange(n, dtype=jnp.int32)[None, :]
    k1 = jnp.arange(n // split, dtype=jnp.int32)[:, None] * split
    k0 = jnp.arange(split, dtype=jnp.int32)[:, None]
    w = 2.0 * math.pi / n
    a = ((k1 * tt) % n).astype(F32) * w
    b = ((k0 * tt) % n).astype(F32) * w
    ca, sa = jnp.cos(a)[:, None, :], jnp.sin(a)[:, None, :]
    cb, sb = jnp.cos(b)[None, :, :], jnp.sin(b)[None, :, :]
    ce = (ca * cb - sa * sb).reshape(n, n)
    se = (sa * cb + ca * sb).reshape(n, n)
    step = jnp.arange(n, dtype=F32)[None, :] * (2.0 * math.pi / t)
    ct, st = jnp.cos(step), jnp.sin(step)
    co = ce * ct - se * st
    so = se * ct + ce * st
    scale = 1.0 / math.sqrt(t * FOURIER_GROUP_DIM)
    return tuple((m * scale).astype(BF16) for m in (ce, se, co, so))


def _fourier_call(u12, mats, tk=512):
    bsz, t, w2 = u12.shape
    w = w2 // 2
    n = t // 2
    tk = min(tk, n)
    mat = pl.BlockSpec((tk, n), lambda i, b: (i, 0))
    return pl.pallas_call(
        _fourier_kernel,
        out_shape=jax.ShapeDtypeStruct((bsz, w // LANES, t, LANES), F32),
        grid=(n // tk, bsz),
        in_specs=[mat, mat, mat, mat, pl.BlockSpec((1, t, w2), lambda i, b: (b, 0, 0))],
        out_specs=pl.BlockSpec((1, w // LANES, 2 * tk, LANES), lambda i, b: (b, 0, i, 0)),
        compiler_params=_cparams(("arbitrary", "arbitrary")),
        name="fourier_dft",
    )(*mats, u12)


def _pack_halves(v):
    n = v.shape[1] // 2
    bits = lax.bitcast_convert_type(v, jnp.uint32)
    return bits[:, :n] | (bits[:, n:] >> 16)


def _unpack_halves(u):
    hi = lax.bitcast_convert_type(u & jnp.uint32(0xFFFF0000), F32)
    lo = lax.bitcast_convert_type(u << 16, F32)
    return jnp.concatenate([hi, lo], axis=1)


ROW_TILE = 8


def _store_token_tiles(ref, first_tok, words):
    n = words.shape[0]
    k = words.shape[1] // LANES
    for c in range(ROW_TILE):
        chunk = words[:, c * LANES:(c + 1) * LANES] if c < k else jnp.zeros((n, LANES), jnp.uint32)
        ref[pl.ds(first_tok * ROW_TILE + c, n, stride=ROW_TILE), :] = chunk


def _load_token_tiles(ref, first_tok, n, k):
    return jnp.concatenate([ref[pl.ds(first_tok * ROW_TILE + c, n, stride=ROW_TILE), :] for c in range(k)], axis=1)


def _route_rows(logits, bias):
    score = jax.nn.sigmoid(logits)
    sel = score + bias
    r = [sel[j:j + 1, :] for j in range(N_EXPERTS)]
    s = [score[j:j + 1, :] for j in range(N_EXPERTS)]
    npg = EXPERTS_PER_GROUP
    best = None
    for g in range(N_GROUPS):
        v = r[g * npg:(g + 1) * npg]
        pair_sums = [v[a] + v[b] for a in range(npg) for b in range(a + 1, npg)]
        tg = functools.reduce(jnp.maximum, pair_sums)
        if best is None:
            best, bg = tg, jnp.zeros_like(tg)
        else:
            upd = tg > best
            best = jnp.where(upd, tg, best)
            bg = jnp.where(upd, float(g), bg)

    def of_group(rows, j):
        out = rows[j]
        for g in range(1, N_GROUPS):
            out = jnp.where(bg == float(g), rows[g * npg + j], out)
        return out

    v = [of_group(r, j) for j in range(npg)]
    sv = [of_group(s, j) for j in range(npg)]

    def first_argmax(vals):
        m = functools.reduce(jnp.maximum, vals)
        idx = jnp.full_like(m, float(npg - 1))
        for j in range(npg - 2, -1, -1):
            idx = jnp.where(vals[j] == m, float(j), idx)
        return idx

    i1 = first_argmax(v)
    i2 = first_argmax([jnp.where(i1 == float(j), -jnp.inf, v[j]) for j in range(npg)])
    lo = jnp.minimum(i1, i2)
    hi = jnp.maximum(i1, i2)

    def pick(vals, idx):
        out = vals[0]
        for j in range(1, npg):
            out = jnp.where(idx == float(j), vals[j], out)
        return out

    s_lo, s_hi = pick(sv, lo), pick(sv, hi)
    den = s_lo + s_hi
    pair = lo * (2.0 * npg - 1.0 - lo) * 0.5 + (hi - lo - 1.0)
    cls = bg * float(N_PAIR_CLASSES // N_GROUPS) + pair
    return cls, s_lo / den, s_hi / den


def _outproj_kernel(f_ref, o_ref, x_ref, ga_ref, lg_ref, lb_ref, sh_ref, sc_ref, w_ref, wr_ref, rb_ref, tri_ref,
                    cin_ref, xo_ref, hx_ref, rt_ref, cnt_ref, *, alpha, n_part):
    first = (pl.program_id(0) == 0) & (pl.program_id(1) == 0)

    @pl.when(first)
    def _():
        cnt_ref[...] = cin_ref[...]

    fw = f_ref.shape[1] * f_ref.shape[3]
    tm = x_ref.shape[1]
    tp = tm // n_part
    logit_parts = []
    packed_parts = []
    projs = []
    for part in range(n_part):
        rows = slice(part * tp, (part + 1) * tp)
        o = jnp.concatenate([o_ref[0, j, rows] for j in range(o_ref.shape[1])], axis=1)
        f = jnp.concatenate([f_ref[0, j, rows] for j in range(f_ref.shape[1])], axis=1).astype(BF16)
        projs.append(_dot(f, w_ref[:fw, :]) + _dot(o, w_ref[fw:, :]))
    for part in range(n_part):
        rows = slice(part * tp, (part + 1) * tp)
        z = alpha * x_ref[0, rows] + ga_ref[0] * projs[part]
        x = _ln(z) * lg_ref[...] + lb_ref[...]
        xo_ref[0, rows] = x
        h = _ln(x) * (1.0 + sc_ref[0]) + sh_ref[0]
        hh, hl = _split_bf16(h)
        packed_parts.append(_pack_halves(hh.astype(F32)))
        r = _dot_nt(wr_ref[...], hh)
        r2 = _dot_nt(wr_ref[:N_EXPERTS, :], hl)
        logit_parts.append(r[:N_EXPERTS] + r[N_EXPERTS:] + r2)
    logits = jnp.concatenate(logit_parts, axis=1)
    cls, w0, w1 = _route_rows(logits, rb_ref[...])

    ncls = cnt_ref.shape[0]
    cls_iota = lax.broadcasted_iota(jnp.int32, (ncls, tm), 0).astype(F32)
    onehot = cls_iota == cls
    oh = onehot.astype(F32)
    prefix = _dot(oh.astype(BF16), tri_ref[...])
    counts = cnt_ref[...]
    base = jnp.concatenate([counts] * (tm // LANES), axis=1)
    rank = jnp.sum(jnp.where(onehot, prefix + base, 0.0), axis=0, keepdims=True)
    cnt_ref[...] = counts + jnp.sum(oh, axis=1, keepdims=True)

    row_iota = lax.broadcasted_iota(jnp.int32, (rt_ref.shape[0], tm), 0)
    rt_ref[...] = jnp.where(row_iota == 0, cls, jnp.where(row_iota == 1, rank, 0.0))

    wrow_iota = lax.broadcasted_iota(jnp.int32, (LANES, tm), 0)
    wcols = jnp.where(wrow_iota == 0, w0, jnp.where(wrow_iota == 1, w1, 0.0)).T
    packed = packed_parts[0] if n_part == 1 else jnp.concatenate(packed_parts, axis=0)
    _store_token_tiles(hx_ref, 0, jnp.concatenate([packed, lax.bitcast_convert_type(wcols, jnp.uint32)], axis=1))


def _outproj_call(f, o, x, g_a, ln_g, ln_b, shift, scale, w_out, wr_hilo, router_bias, counts_in, alpha, tm=512):
    bsz, t, d = x.shape
    tm = min(tm, t)
    nt = t // tm
    assert d // 2 + LANES <= ROW_TILE * LANES
    row = lambda w: pl.BlockSpec((1, tm, w), lambda b, i: (b, i, 0))
    vec = pl.BlockSpec((1, 1, d), lambda b, i: (b, 0, 0))
    cst = lambda shp: pl.BlockSpec(shp, lambda b, i: (0,) * len(shp))
    tri = jnp.asarray(np.triu(np.ones((tm, tm), np.float32), 1), BF16)
    return pl.pallas_call(
        functools.partial(_outproj_kernel, alpha=alpha, n_part=2 if tm % (2 * LANES) == 0 else 1),
        out_shape=(
            jax.ShapeDtypeStruct((bsz, t, d), F32),
            jax.ShapeDtypeStruct((bsz * t * ROW_TILE, LANES), jnp.uint32),
            jax.ShapeDtypeStruct((8, bsz * t), F32),
            jax.ShapeDtypeStruct(counts_in.shape, F32),
        ),
        grid=(bsz, nt),
        in_specs=[pl.BlockSpec((1, f.shape[1], tm, LANES), lambda b, i: (b, 0, i, 0)),
                  pl.BlockSpec((1, o.shape[1], tm, V_DIM), lambda b, i: (b, 0, i, 0)), row(d),
                  vec, cst((1, d)), cst((1, d)), vec, vec, cst(w_out.shape), cst(wr_hilo.shape),
                  cst((N_EXPERTS, 1)), cst((tm, tm)), cst(counts_in.shape)],
        out_specs=(row(d), pl.BlockSpec((tm * ROW_TILE, LANES), lambda b, i: (b * nt + i, 0)),
                   pl.BlockSpec((8, tm), lambda b, i: (0, b * nt + i)), cst(counts_in.shape)),
        compiler_params=_cparams(("arbitrary", "arbitrary")),
        name="outproj",
    )(f, o, x, g_a, ln_g.reshape(1, d), ln_b.reshape(1, d), shift, scale, w_out, wr_hilo,
      router_bias.reshape(N_EXPERTS, 1).astype(F32), tri, counts_in)


COPY_UNROLL = 8


def _token_copies(n_tok, src_tok, dst_tok, sem):
    def start(g, carry):
        for j in range(COPY_UNROLL):
            r = g * COPY_UNROLL + j
            pltpu.make_async_copy(src_tok(r), dst_tok(r), sem).start(priority=j % 2)
        return carry

    def wait(g, carry):
        for j in range(COPY_UNROLL):
            pltpu.make_async_copy(src_tok(0), dst_tok(0), sem).wait()
        return carry

    lax.fori_loop(0, n_tok // COPY_UNROLL, start, 0)
    lax.fori_loop(0, n_tok // COPY_UNROLL, wait, 0)


def _token_tile(ref, tok):
    return ref.at[pl.ds(pl.multiple_of(tok * ROW_TILE, ROW_TILE), ROW_TILE), :]


def _scatter_tokens_kernel(pos_ref, src_ref, dst_in_ref, dst_ref, sem):
    del dst_in_ref
    _token_copies(src_ref.shape[0] // ROW_TILE, lambda r: _token_tile(src_ref, r),
                  lambda r: _token_tile(dst_ref, pos_ref[0, 0, r]), sem)


def _scatter_tokens_call(src, pos, dst, toks_per_step=2048):
    t = pos.shape[0]
    ts = min(toks_per_step, t)
    return pl.pallas_call(
        _scatter_tokens_kernel,
        out_shape=jax.ShapeDtypeStruct(dst.shape, dst.dtype),
        grid=(t // ts,),
        in_specs=[pl.BlockSpec((1, 1, ts), lambda i: (i, 0, 0), memory_space=pltpu.SMEM),
                  pl.BlockSpec((ts * ROW_TILE, LANES), lambda i: (i, 0)),
                  pl.BlockSpec(memory_space=pl.ANY)],
        out_specs=pl.BlockSpec(memory_space=pl.ANY),
        scratch_shapes=[pltpu.SemaphoreType.DMA(())],
        input_output_aliases={2: 0},
        compiler_params=_cparams(("arbitrary",)),
        name="scatter_tokens",
    )(pos.reshape(t // ts, 1, ts), src, dst)


def _zero_tails_kernel(end_ref, dst_ref, zeros_ref, sem):
    zeros_ref[...] = jnp.zeros_like(zeros_ref)
    span = zeros_ref.shape[0] // ROW_TILE

    def tail(c):
        first = jnp.maximum(end_ref[c] - span, 0)
        return dst_ref.at[pl.ds(pl.multiple_of(first * ROW_TILE, ROW_TILE), span * ROW_TILE), :]

    for c in range(end_ref.shape[0]):
        pltpu.make_async_copy(zeros_ref, tail(c), sem).start()
    for c in range(end_ref.shape[0]):
        pltpu.make_async_copy(zeros_ref, tail(c), sem).wait()


def _zero_tails_call(ends, n_slots, span):
    return pl.pallas_call(
        _zero_tails_kernel,
        out_shape=jax.ShapeDtypeStruct((n_slots * ROW_TILE, LANES), jnp.uint32),
        in_specs=[pl.BlockSpec(memory_space=pltpu.SMEM)],
        out_specs=pl.BlockSpec(memory_space=pl.ANY),
        scratch_shapes=[pltpu.VMEM((span * ROW_TILE, LANES), jnp.uint32), pltpu.SemaphoreType.DMA(())],
        compiler_params=_cparams(None),
        name="zero_tails",
    )(ends)


def _gather_tokens_kernel(pos_ref, src_ref, dst_ref, sem):
    _token_copies(dst_ref.shape[0] // ROW_TILE, lambda r: _token_tile(src_ref, pos_ref[0, 0, r]),
                  lambda r: _token_tile(dst_ref, r), sem)


def _gather_tokens_call(src, pos, toks_per_step=2048):
    t = pos.shape[0]
    ts = min(toks_per_step, t)
    return pl.pallas_call(
        _gather_tokens_kernel,
        out_shape=jax.ShapeDtypeStruct((t * ROW_TILE, LANES), src.dtype),
        grid=(t // ts,),
        in_specs=[pl.BlockSpec((1, 1, ts), lambda i: (i, 0, 0), memory_space=pltpu.SMEM),
                  pl.BlockSpec(memory_space=pl.ANY)],
        out_specs=pl.BlockSpec((ts * ROW_TILE, LANES), lambda i: (i, 0)),
        scratch_shapes=[pltpu.SemaphoreType.DMA(())],
        compiler_params=_cparams(("arbitrary",)),
        name="gather_tokens",
    )(pos.reshape(t // ts, 1, ts), src)


def _moe_kernel(e0_ref, e1_ref, src_ref, fresh_ref, x_ref, wg0, wu0, wd0, wg1, wu1, wd1, y_ref,
                wg_s, wu_s, wd_s):
    i = pl.program_id(0)
    half = wg_s.shape[1] // 2
    valid = src_ref[i] == i

    @pl.when(fresh_ref[i] != 0)
    def _():
        for slot, (wg, wu, wd) in enumerate(((wg0, wu0, wd0), (wg1, wu1, wd1))):
            wg_s[slot] = wg[0, 0].astype(BF16)
            wu_s[slot] = wu[0, 0].astype(BF16)
            wd_s[slot] = wd[0, 0].astype(BF16)

    @pl.when(valid)
    def _():
        n_tok = x_ref.shape[0] // ROW_TILE
        xrow = _load_token_tiles(x_ref, 0, n_tok, half // LANES + 1)
        x = _unpack_halves(xrow[:, :half]).astype(BF16)
        wt = lax.bitcast_convert_type(xrow[:, half:], F32)
        gu = [(_dot(x, wg_s[slot]), _dot(x, wu_s[slot])) for slot in range(2)]
        ys = [_dot((g * jax.nn.sigmoid(g) * u).astype(BF16), wd_s[slot]) for slot, (g, u) in enumerate(gu)]
        y = wt[:, 0:1] * ys[0] + wt[:, 1:2] * ys[1]
        _store_token_tiles(y_ref, 0, _pack_halves(y.astype(BF16).astype(F32)))

    @pl.when(jnp.logical_not(valid))
    def _():
        y_ref[...] = jnp.zeros_like(y_ref)


def _moe_call(xs, tile_e0, tile_e1, tile_src, tile_fresh, w_gate, w_up, w_down, layer, tmm):
    tpad = xs.shape[0] // ROW_TILE
    _, _, d, de = w_gate.shape
    n_tiles = tpad // tmm
    tok_tiles = pl.BlockSpec((tmm * ROW_TILE, LANES), lambda i, e0, e1, vl, fr: (i, 0))
    src_tiles = pl.BlockSpec((tmm * ROW_TILE, LANES), lambda i, e0, e1, vl, fr: (vl[i], 0))
    wspec0 = lambda shp: pl.BlockSpec((1, 1) + shp, lambda i, e0, e1, vl, fr: (layer, e0[i], 0, 0))
    wspec1 = lambda shp: pl.BlockSpec((1, 1) + shp, lambda i, e0, e1, vl, fr: (layer, e1[i], 0, 0))
    grid_spec = pltpu.PrefetchScalarGridSpec(
        num_scalar_prefetch=4,
        grid=(n_tiles,),
        in_specs=[
            src_tiles,
            wspec0((d, de)), wspec0((d, de)), wspec0((de, d)),
            wspec1((d, de)), wspec1((d, de)), wspec1((de, d)),
        ],
        out_specs=tok_tiles,
        scratch_shapes=[pltpu.VMEM((2, d, de), BF16), pltpu.VMEM((2, d, de), BF16), pltpu.VMEM((2, de, d), BF16)],
    )
    return pl.pallas_call(
        _moe_kernel,
        out_shape=jax.ShapeDtypeStruct(xs.shape, jnp.uint32),
        grid_spec=grid_spec,
        compiler_params=_cparams(("arbitrary",)),
        name="moe_ffn",
    )(tile_e0, tile_e1, tile_src, tile_fresh, xs, w_gate, w_up, w_down, w_gate, w_up, w_down)


def _pair_tables():
    e0s, e1s = [], []
    for g in range(N_GROUPS):
        for a in range(EXPERTS_PER_GROUP):
            for b in range(a + 1, EXPERTS_PER_GROUP):
                e0s.append(g * EXPERTS_PER_GROUP + a)
                e1s.append(g * EXPERTS_PER_GROUP + b)
    return np.asarray(e0s, np.int32), np.asarray(e1s, np.int32)


def _moe_ffn(rows, routes, counts, w_gate, w_up, w_down, layer, tmm=512):
    t_all = sum(rt.shape[1] for rt in routes)
    cnt = counts[:N_PAIR_CLASSES, 0].astype(jnp.int32)
    padded = ((cnt + tmm - 1) // tmm) * tmm
    ends = jnp.cumsum(padded)
    starts = ends - padded
    n_tiles = -(-t_all // tmm) + N_PAIR_CLASSES
    tpad = n_tiles * tmm
    total = ends[-1]
    tile_start = jnp.arange(n_tiles, dtype=jnp.int32) * tmm
    n_real = total // tmm
    tile_ids = jnp.arange(n_tiles, dtype=jnp.int32)
    tile_src = jnp.where(tile_ids < n_real, tile_ids, n_real - 1)
    probe = jnp.minimum(tile_start, total - 1)
    tile_cls = jnp.sum((ends[None, :] <= probe[:, None]).astype(jnp.int32), axis=1)
    tile_cls = jnp.minimum(tile_cls, N_PAIR_CLASSES - 1)
    pe0, pe1 = _pair_tables()
    tile_e0 = jnp.asarray(pe0)[tile_cls]
    tile_e1 = jnp.asarray(pe1)[tile_cls]
    tile_fresh = jnp.concatenate([jnp.ones((1,), jnp.int32),
                                  (tile_cls[1:] != tile_cls[:-1]).astype(jnp.int32)])
    cls_ids = jnp.arange(N_PAIR_CLASSES, dtype=F32)[:, None]
    starts_f = starts.astype(F32)[:, None]
    poss = []
    for rt in routes:
        start_of_tok = jnp.sum(jnp.where(rt[0][None, :] == cls_ids, starts_f, 0.0), axis=0)
        poss.append((start_of_tok + rt[1]).astype(jnp.int32))
    xs = _zero_tails_call(ends, tpad, tmm)
    for r, pos in zip(rows, poss):
        xs = _scatter_tokens_call(r, pos, xs)
    ys = _moe_call(xs, tile_e0, tile_e1, tile_src, tile_fresh, w_gate, w_up, w_down, layer, tmm)
    return [_gather_tokens_call(ys, pos) for pos in poss]


def _ffn_ln_kernel(x_ref, y_ref, gf_ref, lg_ref, lb_ref, o_ref, *, alpha):
    tm, d = x_ref.shape[1], x_ref.shape[2]
    y = _unpack_halves(_load_token_tiles(y_ref, 0, tm, d // 2 // LANES))
    z = alpha * x_ref[0] + gf_ref[0] * y
    o_ref[0] = _ln(z) * lg_ref[...] + lb_ref[...]


def _ffn_ln_call(x, y, g_f, ln_g, ln_b, alpha, tm=512):
    bsz, t, d = x.shape
    tm = min(tm, t)
    row = pl.BlockSpec((1, tm, d), lambda b, i: (b, i, 0))
    vec = pl.BlockSpec((1, 1, d), lambda b, i: (b, 0, 0))
    cst = pl.BlockSpec((1, d), lambda b, i: (0, 0))
    return pl.pallas_call(
        functools.partial(_ffn_ln_kernel, alpha=alpha),
        out_shape=jax.ShapeDtypeStruct((bsz, t, d), F32),
        grid=(bsz, t // tm),
        in_specs=[row, pl.BlockSpec((tm * ROW_TILE, LANES), lambda b, i: (b * (t // tm) + i, 0)), vec, cst, cst],
        out_specs=row,
        compiler_params=_cparams(("arbitrary", "arbitrary")),
        name="ffn_ln",
    )(x, y, g_f, ln_g.reshape(1, d), ln_b.reshape(1, d))


def _rope_tables(s):
    rows = s // GRID_W
    row = jnp.repeat(jnp.arange(rows), GRID_W).astype(F32)
    col = jnp.tile(jnp.arange(GRID_W), rows).astype(F32)
    freqs = ROPE_BASE ** (-jnp.arange(0, ROPE_AXIS_DIM, 2, dtype=F32) / ROPE_AXIS_DIM)
    ang_row = row[:, None] * freqs
    ang_col = col[:, None] * freqs

    def head(r, c, sign):
        return jnp.concatenate([sign * r, r, sign * c, c], axis=1)

    cos = head(jnp.cos(ang_row), jnp.cos(ang_col), 1.0)
    sin = head(jnp.sin(ang_row), jnp.sin(ang_col), -1.0)
    cos = jnp.concatenate([cos, cos], axis=1)
    sin = jnp.concatenate([sin, sin], axis=1)
    return cos, sin


def kernel(x, c, ctx, c_ctx, w_mod, b_mod, w_in, w_fourier, lam_qk, subln_g, w_out, ln_attn_g, ln_attn_b,
           ln_ffn_g, ln_ffn_b, w_router, router_bias, w_gate, w_up, w_down):
    bsz, s, d = x.shape
    l_ctx = ctx.shape[1]
    depth = w_mod.shape[0]
    alpha = (2 * depth) ** 0.25
    qscale = LOG2E * HEAD_DIM ** -0.5

    pad = (-(bsz + 1)) % 8
    cc = jnp.concatenate([c, c_ctx[None, :], jnp.zeros((pad, d), F32)], axis=0)
    mod = _mod_call(cc, w_mod, b_mod)

    ab = _fprep_call(w_fourier)
    v_off = w_in.shape[2] - (w_in.shape[2] - FOURIER_WIDTH) // 3
    w_uqk = w_in[:, :, :v_off].astype(BF16)
    w_vt = jnp.swapaxes(w_in[:, :, v_off:], 1, 2).astype(BF16)
    w_out_b = w_out.astype(BF16)
    wr_t = w_router.T.astype(F32)
    wr_hi = wr_t.astype(BF16)
    wr_lo = (wr_t - wr_hi.astype(F32)).astype(BF16)
    wr_hilo = jnp.concatenate([wr_hi, wr_lo], axis=0)

    cos, sin = _rope_tables(s)
    tabs_lat = (cos * qscale, sin * qscale, cos, sin)
    ones = jnp.ones((l_ctx, LANES), F32)
    zeros = jnp.zeros((l_ctx, LANES), F32)
    tabs_ctx = (ones * qscale, zeros, ones, zeros)
    dft_lat = _dft_mats(s)
    dft_ctx = _dft_mats(l_ctx)

    xc = ctx
    prev = None
    prev_c = None
    for l in range(depth):
        last = l == depth - 1
        lam_init = 0.8 - 0.6 * math.exp(-0.3 * l)
        lq = lam_qk[l].astype(F32)
        lam = jnp.exp(jnp.sum(lq[0] * lq[1])) - jnp.exp(jnp.sum(lq[2] * lq[3])) + lam_init

        m_lat = mod[l, :bsz].reshape(bsz, 1, 6, d)
        sh_a, sc_a, g_a, sh_f, sc_f, g_f = [m_lat[:, :, i, :] for i in range(6)]
        m_ctx = jnp.broadcast_to(mod[l, bsz].reshape(1, 1, 6, d), (bsz, 1, 6, d))
        csh_a, csc_a, cg_a, csh_f, csc_f, cg_f = [m_ctx[:, :, i, :] for i in range(6)]

        outs = _inproj_call(x, sh_a, sc_a, w_uqk[l], w_vt[l], ab[l], tabs_lat, prev=prev, alpha=alpha)
        u12, q, k, v = outs[:4]
        if prev is not None:
            x = outs[4]
        outs_c = _inproj_call(xc, csh_a, csc_a, w_uqk[l], w_vt[l], ab[l], tabs_ctx, prev=prev_c, alpha=alpha)
        u12c, qc, kc, vc = outs_c[:4]
        if prev_c is not None:
            xc = outs_c[4]

        o_attn = _attn_call(q, [(kc, vc), (k, v)], lam, subln_g[l], lam_init)
        f = _fourier_call(u12, dft_lat)
        counts = jnp.zeros((32, LANES), F32)
        rows, routes = [], []
        if not last:
            oc_attn = _attn_call(qc, [(kc, vc)], lam, subln_g[l], lam_init)
            fc = _fourier_call(u12c, dft_ctx)
            xc, hxc, route_c, counts = _outproj_call(fc, oc_attn, xc, cg_a, ln_attn_g[l], ln_attn_b[l], csh_f,
                                                     csc_f, w_out_b[l], wr_hilo, router_bias, counts, alpha)
            rows.append(hxc)
            routes.append(route_c)
        x, hx, route_t, counts = _outproj_call(f, o_attn, x, g_a, ln_attn_g[l], ln_attn_b[l], sh_f, sc_f,
                                               w_out_b[l], wr_hilo, router_bias, counts, alpha)
        rows.append(hx)
        routes.append(route_t)

        ys = _moe_ffn(rows, routes, counts, w_gate, w_up, w_down, l)
        if not last:
            prev_c = (ys[0], cg_f, ln_ffn_g[l], ln_ffn_b[l])
        prev = (ys[-1], g_f, ln_ffn_g[l], ln_ffn_b[l])

    y, g_f, lg, lb = prev
    return _ffn_ln_call(x, y, g_f, lg, lb, alpha)
```
